```python
import math
import jax, jax.numpy as jnp
from jax import lax
import numpy as np

D_MODEL = 1024
BATCH = 8
SEQ = 4096
DEPTH = 4

GRID_W = 64
CTX_LEN = 256
N_MIXERS = 3
N_ATTN_LAYERS = len(range(0, DEPTH, N_MIXERS))
N_CONF_LAYERS = len(range(1, DEPTH, N_MIXERS))
N_SCONV_LAYERS = len(range(2, DEPTH, N_MIXERS))
N_HEADS = 8
HEAD_DIM = D_MODEL // (2 * N_HEADS)
V_HEAD_DIM = 2 * HEAD_DIM
Q_BLOCK = 128
ROPE_THETA = 10000.0
D_FF = ((8 * D_MODEL // 3 + 127) // 128) * 128
CONF_KERNEL = 31
SCONV_KERNEL = 3
N_MOD = 9
EPS = 1e-6
LAMBDA_STD = 0.1

kernel_name = "hybrid_diffattn_conformer_shortconv_dit"


def rmsnorm(x, g):
    x32 = x.astype(jnp.float32)
    y = x32 * lax.rsqrt(jnp.mean(x32 * x32, axis=-1, keepdims=True) + EPS)
    return (y * g.astype(jnp.float32)).astype(x.dtype)


def layernorm(x, g, b):
    x32 = x.astype(jnp.float32)
    mu = jnp.mean(x32, axis=-1, keepdims=True)
    var = jnp.mean(jnp.square(x32 - mu), axis=-1, keepdims=True)
    y = (x32 - mu) * lax.rsqrt(var + EPS)
    return (y * g.astype(jnp.float32) + b.astype(jnp.float32)).astype(x.dtype)


def ada_in(s, g, shift, scale):
    return rmsnorm(s, g) * (1 + scale) + shift


def swiglu(h, w_in, w_out):
    a, gt = jnp.split(h @ w_in, 2, axis=-1)
    return (jax.nn.silu(gt) * a) @ w_out


def depthwise_conv(u, w):
    return lax.conv_general_dilated(u, w[:, None, :], window_strides=(1,), padding='SAME',
                                    dimension_numbers=('NWC', 'WIO', 'NWC'),
                                    feature_group_count=u.shape[-1])


def axial_rope_tables(n_tokens, dtype):
    rows = n_tokens // GRID_W
    row_ids = jnp.repeat(jnp.arange(rows, dtype=jnp.float32), GRID_W)
    col_ids = jnp.tile(jnp.arange(GRID_W, dtype=jnp.float32), rows)
    half = HEAD_DIM // 2
    inv_freq = ROPE_THETA ** (-jnp.arange(0, half, 2, dtype=jnp.float32) / half)
    ang_r = row_ids[:, None] * inv_freq
    ang_c = col_ids[:, None] * inv_freq
    ang = jnp.concatenate([ang_r, ang_r, ang_c, ang_c], axis=-1)
    return jnp.cos(ang).astype(dtype), jnp.sin(ang).astype(dtype)


def rope2d(x, cos, sin):
    x1, x2, x3, x4 = jnp.split(x, 4, axis=-1)
    rot = jnp.concatenate([-x2, x1, -x4, x3], axis=-1)
    return x * cos[:, None, None, :] + rot * sin[:, None, None, :]


def diff_softmax_mix(q, k, v, lam):
    s = jnp.einsum('bhmqd,bhmkd->bhmqk', q, k).astype(jnp.float32) * (HEAD_DIM ** -0.5)
    p = jax.nn.softmax(s, axis=-1)
    a = p[:, :, 0] - lam * p[:, :, 1]
    return jnp.einsum('bhqk,bhkd->bhqd', a.astype(v.dtype), v)


def diff_attention(h_lat, h_ctx, w_qkv, w_o, q_g, k_g, lam_p, subln_g, lam_init, cos, sin, with_ctx_out):
    B, L, D = h_lat.shape

    def project(h, use_rope):
        n = h.shape[1]
        q, k, v = jnp.split(h @ w_qkv, 3, axis=-1)
        q = rmsnorm(q.reshape(B, n, N_HEADS, 2, HEAD_DIM), q_g)
        k = rmsnorm(k.reshape(B, n, N_HEADS, 2, HEAD_DIM), k_g)
        if use_rope:
            q = rope2d(q, cos, sin)
            k = rope2d(k, cos, sin)
        q = q.transpose(0, 2, 3, 1, 4)
        k = k.transpose(0, 2, 3, 1, 4)
        v = v.reshape(B, n, N_HEADS, V_HEAD_DIM).transpose(0, 2, 1, 3)
        return q, k, v

    lp = lam_p.astype(jnp.float32)
    lam = jnp.exp(jnp.sum(lp[0] * lp[1])) - jnp.exp(jnp.sum(lp[2] * lp[3])) + lam_init

    q_l, k_l, v_l = project(h_lat, True)
    q_c, k_c, v_c = project(h_ctx, False)
    k_all = jnp.concatenate([k_c, k_l], axis=3)
    v_all = jnp.concatenate([v_c, v_l], axis=2)

    n_blk = L // Q_BLOCK
    q_blocks = jnp.moveaxis(q_l.reshape(B, N_HEADS, 2, n_blk, Q_BLOCK, HEAD_DIM), 3, 0)
    o = lax.map(lambda qb: diff_softmax_mix(qb, k_all, v_all, lam), q_blocks)
    o = jnp.moveaxis(o, 0, 2).reshape(B, N_HEADS, L, V_HEAD_DIM)

    def finish(o):
        n = o.shape[2]
        o = rmsnorm(o, subln_g) * (1.0 - lam_init)
        return o.transpose(0, 2, 1, 3).reshape(B, n, D) @ w_o

    out_lat = finish(o)
    out_ctx = finish(diff_softmax_mix(q_c, k_c, v_c, lam)) if with_ctx_out else None
    return out_lat, out_ctx


def conformer_conv(h, w_in, b_in, dw_w, dw_b, ln_g, ln_b, w_out, b_out):
    a, g = jnp.split(h @ w_in + b_in, 2, axis=-1)
    u = a * jax.nn.sigmoid(g)
    u = depthwise_conv(u, dw_w) + dw_b
    u = jax.nn.silu(layernorm(u, ln_g, ln_b))
    return u @ w_out + b_out


def short_conv(h, w_in, dw_w, w_out):
    b, cg, xh = jnp.split(h @ w_in, 3, axis=-1)
    return (b * depthwise_conv(cg * xh, dw_w)) @ w_out


def setup_inputs(seed: int = 0) -> dict:
    key = jax.random.key(seed)
    ks = iter(jax.random.split(key, 32))
    D, F = D_MODEL, D_FF

    def w(shape, fan_in):
        return jax.random.normal(next(ks), shape, jnp.float32) * fan_in ** -0.5

    def gain(shape):
        return 1.0 + 0.02 * jax.random.normal(next(ks), shape, jnp.float32)

    def bias(shape):
        return 0.01 * jax.random.normal(next(ks), shape, jnp.float32)

    return {
        "x": jax.random.normal(next(ks), (BATCH, SEQ, D), jnp.float32),
        "c": jax.random.normal(next(ks), (BATCH, D), jnp.float32),
        "ctx": jax.random.normal(next(ks), (BATCH, CTX_LEN, D), jnp.float32),
        "c_ctx": jax.random.normal(next(ks), (D,), jnp.float32),
        "ada_w": w((DEPTH, D, N_MOD * D), D),
        "ada_b": bias((DEPTH, N_MOD * D)),
        "norm_g": gain((DEPTH, 3, D)),
        "ffn_w_in": w((DEPTH, 2, D, 2 * F), D),
        "ffn_w_out": w((DEPTH, 2, F, D), F),
        "attn_w_qkv": w((N_ATTN_LAYERS, D, 3 * D), D),
        "attn_w_o": w((N_ATTN_LAYERS, D, D), D),
        "attn_q_g": gain((N_ATTN_LAYERS, HEAD_DIM)),
        "attn_k_g": gain((N_ATTN_LAYERS, HEAD_DIM)),
        "attn_lambda": LAMBDA_STD * jax.random.normal(next(ks), (N_ATTN_LAYERS, 4, HEAD_DIM), jnp.float32),
        "attn_subln_g": gain((N_ATTN_LAYERS, V_HEAD_DIM)),
        "conv_w_in": w((N_CONF_LAYERS, D, 2 * D), D),
        "conv_b_in": bias((N_CONF_LAYERS, 2 * D)),
        "conv_dw_w": w((N_CONF_LAYERS, CONF_KERNEL, D), CONF_KERNEL),
        "conv_dw_b": bias((N_CONF_LAYERS, D)),
        "conv_ln_g": gain((N_CONF_LAYERS, D)),
        "conv_ln_b": bias((N_CONF_LAYERS, D)),
        "conv_w_out": w((N_CONF_LAYERS, D, D), D),
        "conv_b_out": bias((N_CONF_LAYERS, D)),
        "sc_w_in": w((N_SCONV_LAYERS, D, 3 * D), D),
        "sc_dw_w": w((N_SCONV_LAYERS, SCONV_KERNEL, D), SCONV_KERNEL),
        "sc_w_out": w((N_SCONV_LAYERS, D, D), D),
    }


def reference(x, c, ctx, c_ctx, ada_w, ada_b, norm_g, ffn_w_in, ffn_w_out,
              attn_w_qkv, attn_w_o, attn_q_g, attn_k_g, attn_lambda, attn_subln_g,
              conv_w_in, conv_b_in, conv_dw_w, conv_dw_b, conv_ln_g, conv_ln_b, conv_w_out, conv_b_out,
              sc_w_in, sc_dw_w, sc_w_out):
    L = x.shape[1]
    cos, sin = axial_rope_tables(L, x.dtype)
    sc_ = jax.nn.silu(c)
    sc_ctx = jax.nn.silu(c_ctx)

    for i in range(DEPTH):
        kind = i % N_MIXERS
        j = i // N_MIXERS
        last = i == DEPTH - 1
        ctx_in_needed = (not last) or kind == 0
        ctx_out_needed = not last

        ml = [m[:, None, :] for m in jnp.split(sc_ @ ada_w[i] + ada_b[i], N_MOD, axis=-1)]
        mc = jnp.split(sc_ctx @ ada_w[i] + ada_b[i], N_MOD, axis=-1)

        x = x + 0.5 * ml[2] * swiglu(ada_in(x, norm_g[i, 0], ml[0], ml[1]), ffn_w_in[i, 0], ffn_w_out[i, 0])
        if ctx_in_needed:
            ctx = ctx + 0.5 * mc[2] * swiglu(ada_in(ctx, norm_g[i, 0], mc[0], mc[1]), ffn_w_in[i, 0], ffn_w_out[i, 0])

        hl = ada_in(x, norm_g[i, 1], ml[3], ml[4])
        hc = ada_in(ctx, norm_g[i, 1], mc[3], mc[4]) if ctx_in_needed else None
        if kind == 0:
            lam_init = 0.8 - 0.6 * math.exp(-0.3 * i)
            ol, oc = diff_attention(hl, hc, attn_w_qkv[j], attn_w_o[j], attn_q_g[j], attn_k_g[j],
                                    attn_lambda[j], attn_subln_g[j], lam_init, cos, sin, ctx_out_needed)
        elif kind == 1:
            cp = (conv_w_in[j], conv_b_in[j], conv_dw_w[j], conv_dw_b[j], conv_ln_g[j], conv_ln_b[j],
                  conv_w_out[j], conv_b_out[j])
            ol = conformer_conv(hl, *cp)
            oc = conformer_conv(hc, *cp) if ctx_out_needed else None
        else:
            ol = short_conv(hl, sc_w_in[j], sc_dw_w[j], sc_w_out[j])
            oc = short_conv(hc, sc_w_in[j], sc_dw_w[j], sc_w_out[j]) if ctx_out_needed else None
        x = x + ml[5] * ol
        if ctx_out_needed:
            ctx = ctx + mc[5] * oc

        x = x + 0.5 * ml[8] * swiglu(ada_in(x, norm_g[i, 2], ml[6], ml[7]), ffn_w_in[i, 1], ffn_w_out[i, 1])
        if ctx_out_needed:
            ctx = ctx + 0.5 * mc[8] * swiglu(ada_in(ctx, norm_g[i, 2], mc[6], mc[7]), ffn_w_in[i, 1], ffn_w_out[i, 1])

    return x
```

```python
import functools
import math

import jax
import jax.numpy as jnp
from jax import lax
from jax.experimental import pallas as pl
from jax.experimental.pallas import tpu as pltpu

GRID_W = 64
N_MIXERS = 3
N_HEADS = 8
N_MOD = 9
ROPE_THETA = 10000.0
EPS = 1e-6

LANES = 128
MXU_DIM = 256
VMEM_LIMIT = 56 * 1024 * 1024

BF16 = jnp.bfloat16
F32 = jnp.float32


def _dot(a, b):
    return jnp.dot(a, b, preferred_element_type=F32)


def _dot_nt(a, b):
    return lax.dot_general(a, b, (((1,), (1,)), ((), ())), preferred_element_type=F32)


def _sigmoid(x):
    return 1.0 / (1.0 + jnp.exp(-x))


def _ada_in(x, g, shift, scale):
    y = x * lax.rsqrt(jnp.mean(x * x, axis=-1, keepdims=True) + EPS)
    return (y * g) * (1.0 + scale) + shift


def _mod_rows(mod_ref, base):
    return (mod_ref[base:base + 1, :], mod_ref[base + 1:base + 2, :], mod_ref[base + 2:base + 3, :])


def _tok_spec(tm, d):
    return pl.BlockSpec((None, tm, d), lambda b, i: (b, i, 0))


def _mod_spec(mods):
    per_batch = mods.shape[0] > 1
    return pl.BlockSpec((None,) + mods.shape[1:], lambda b, i: (b if per_batch else 0, 0, 0))


def _const_spec(arr):
    nd = arr.ndim
    return pl.BlockSpec(arr.shape, lambda *_: (0,) * nd, pipeline_mode=pl.Buffered(1))


def _params(n_grid):
    return pltpu.CompilerParams(dimension_semantics=("parallel",) * n_grid,
                                vmem_limit_bytes=VMEM_LIMIT)


def _token_tile(n):
    return 512 if n % 512 == 0 else n


def _mods_kernel(c_ref, w_ref, b_ref, o_ref):
    c = c_ref[...]
    sc = (c * _sigmoid(c)).astype(BF16)
    o_ref[...] = _dot(sc, w_ref[...].astype(BF16)) + b_ref[...]


def _modulations(cvec, ada_w, ada_b):
    depth, d, n = ada_w.shape
    r = cvec.shape[0]
    tn = n // 8
    return pl.pallas_call(
        _mods_kernel,
        grid=(depth, n // tn),
        in_specs=[pl.BlockSpec((r, d), lambda l, j: (0, 0)),
                  pl.BlockSpec((None, d, tn), lambda l, j: (l, 0, j)),
                  pl.BlockSpec((None, 1, tn), lambda l, j: (l, 0, j))],
        out_specs=pl.BlockSpec((None, r, tn), lambda l, j: (l, 0, j)),
        out_shape=jax.ShapeDtypeStruct((depth, r, n), F32),
        compiler_params=_params(2),
        name="adaln_mods",
    )(cvec, ada_w, ada_b.reshape(depth, 1, n))


def _ffn_kernel(x_ref, mod_ref, g_ref, win_ref, wout_ref, o_ref, u_ref, *, mod_base, fc):
    x = x_ref[...]
    shift, scale, gate = _mod_rows(mod_ref, mod_base)
    h = _ada_in(x, g_ref[...], shift, scale).astype(BF16)
    f = wout_ref.shape[0]
    for c in range(f // fc):
        a = _dot(h, win_ref[:, c * fc:(c + 1) * fc])
        gt = _dot(h, win_ref[:, f + c * fc:f + (c + 1) * fc])
        u_ref[:, c * fc:(c + 1) * fc] = ((gt * _sigmoid(gt)) * a).astype(BF16)
    y = _dot(u_ref[...], wout_ref[...])
    o_ref[...] = x + (0.5 * gate) * y


def _ffn(x, mods, g, w_in, w_out, mod_base):
    b, n, d = x.shape
    f = w_out.shape[0]
    tm = _token_tile(n)
    kern = functools.partial(_ffn_kernel, mod_base=mod_base, fc=MXU_DIM)
    return pl.pallas_call(
        kern,
        grid=(b, n // tm),
        in_specs=[_tok_spec(tm, d), _mod_spec(mods), _const_spec(g), _const_spec(w_in), _const_spec(w_out)],
        out_specs=_tok_spec(tm, d),
        out_shape=jax.ShapeDtypeStruct(x.shape, F32),
        scratch_shapes=[pltpu.VMEM((tm, f), BF16)],
        compiler_params=_params(2),
        name="ffn",
    )(x, mods, g, w_in, w_out)


def _qkv_kernel(*refs, mod_base, use_rope):
    if use_rope:
        (x_ref, mod_ref, g_ref, w_ref, gm_ref, qg_ref, kg_ref, cos_ref, sin_ref,
         q_ref, k_ref, vt_ref) = refs
    else:
        x_ref, mod_ref, g_ref, w_ref, gm_ref, qg_ref, kg_ref, q_ref, k_ref, vt_ref = refs
    x = x_ref[...]
    d = x.shape[1]
    shift, scale, _ = _mod_rows(mod_ref, mod_base)
    h = _ada_in(x, g_ref[...], shift, scale).astype(BF16)
    qkv = _dot(h, w_ref[...])

    if use_rope:
        cos = cos_ref[...]
        sin = sin_ref[...]
        lane = lax.broadcasted_iota(jnp.int32, cos.shape, 1)
        low_half = (lane % 32) < 16

    def norm_rope(t, gain_ref, out_ref, post_scale):
        for p in range(d // MXU_DIM):
            sl = t[:, p * MXU_DIM:(p + 1) * MXU_DIM]
            ms = _dot((sl * sl).astype(BF16), gm_ref[...])
            sl = sl * lax.rsqrt(ms + EPS) * gain_ref[...]
            for hh in range(MXU_DIM // LANES):
                y = sl[:, hh * LANES:(hh + 1) * LANES]
                if use_rope:
                    rot = jnp.where(low_half, pltpu.roll(y, LANES - 16, 1), pltpu.roll(y, 16, 1))
                    y = y * cos + rot * sin
                c0 = p * MXU_DIM + hh * LANES
                out_ref[:, c0:c0 + LANES] = (y * post_scale).astype(BF16)

    head_dim = d // (2 * N_HEADS)
    norm_rope(qkv[:, :d], qg_ref, q_ref, head_dim ** -0.5)
    norm_rope(qkv[:, d:2 * d], kg_ref, k_ref, 1.0)
    vt_ref[...] = qkv[:, 2 * d:].T.astype(BF16)


def _qkv_proj(x, mods, g, w_qkv, gmat, qg, kg, rope, mod_base):
    b, n, d = x.shape
    tm = _token_tile(n)
    use_rope = rope is not None
    ins = [x, mods, g, w_qkv, gmat, qg, kg]
    specs = [_tok_spec(tm, d), _mod_spec(mods), _const_spec(g), _const_spec(w_qkv), _const_spec(gmat),
             _const_spec(qg), _const_spec(kg)]
    if use_rope:
        ins += list(rope)
        specs += [pl.BlockSpec((tm, LANES), lambda bb, i: (i, 0))] * 2
    kern = functools.partial(_qkv_kernel, mod_base=mod_base, use_rope=use_rope)
    return pl.pallas_call(
        kern,
        grid=(b, n // tm),
        in_specs=specs,
        out_specs=[_tok_spec(tm, d), _tok_spec(tm, d),
                   pl.BlockSpec((None, d, tm), lambda bb, i: (bb, 0, i))],
        out_shape=[jax.ShapeDtypeStruct((b, n, d), BF16), jax.ShapeDtypeStruct((b, n, d), BF16),
                   jax.ShapeDtypeStruct((b, d, n), BF16)],
        compiler_params=_params(2),
        name="qkv_proj",
    )(*ins)


def _flash_kernel(*refs, n_seg, tk, lam_init):
    lamp_ref, subg_ref, q_ref = refs[:3]
    seg_refs = refs[3:3 + 2 * n_seg]
    o_ref = refs[3 + 2 * n_seg]

    lp = lamp_ref[...]
    lam = (jnp.exp(jnp.sum(lp[0:1] * lp[1:2], keepdims=True))
           - jnp.exp(jnp.sum(lp[2:3] * lp[3:4], keepdims=True)) + lam_init)

    q = q_ref[...]
    tq, vd = q.shape
    lane = lax.broadcasted_iota(jnp.int32, q.shape, 1)
    zero = jnp.zeros_like(q)
    qm = (jnp.where(lane < vd // 2, q, zero), jnp.where(lane >= vd // 2, q, zero))

    m = [jnp.full((1, tq), -jnp.inf, F32) for _ in range(2)]
    l = [jnp.zeros((1, tq), F32) for _ in range(2)]
    acc = [jnp.zeros((vd, tq), F32) for _ in range(2)]
    for s in range(n_seg):
        k_ref, vt_ref = seg_refs[2 * s], seg_refs[2 * s + 1]
        nk = k_ref.shape[0]
        ck = min(tk, nk)
        for c in range(nk // ck):
            kch = k_ref[c * ck:(c + 1) * ck, :]
            vtch = vt_ref[:, c * ck:(c + 1) * ck]
            for j in range(2):
                st = _dot_nt(kch, qm[j])
                m_new = jnp.maximum(m[j], jnp.max(st, axis=0, keepdims=True))
                alpha = jnp.exp(m[j] - m_new)
                p = jnp.exp(st - m_new)
                l[j] = alpha * l[j] + jnp.sum(p, axis=0, keepdims=True)
                acc[j] = alpha * acc[j] + _dot(vtch, p.astype(BF16))
                m[j] = m_new

    ot = acc[0] / l[0] - lam * (acc[1] / l[1])
    ot = ot * lax.rsqrt(jnp.mean(ot * ot, axis=0, keepdims=True) + EPS)
    ot = ot * subg_ref[...] * (1.0 - lam_init)
    o_ref[...] = ot.T.astype(BF16)


def _flash(q, segs, lam_p, subg, lam_init, tq, tk):
    b, lq, d = q.shape
    vd = d // N_HEADS
    tq = min(tq, lq)
    ins = [lam_p, subg, q]
    specs = [pl.BlockSpec(lam_p.shape, lambda bb, h, i: (0, 0)),
             pl.BlockSpec(subg.shape, lambda bb, h, i: (0, 0)),
             pl.BlockSpec((None, tq, vd), lambda bb, h, i: (bb, i, h))]
    for k, vt in segs:
        nk = k.shape[1]
        ins += [k, vt]
        specs += [pl.BlockSpec((None, nk, vd), lambda bb, h, i: (bb, 0, h)),
                  pl.BlockSpec((None, vd, nk), lambda bb, h, i: (bb, h, 0))]
    kern = functools.partial(_flash_kernel, n_seg=len(segs), tk=tk, lam_init=lam_init)
    return pl.pallas_call(
        kern,
        grid=(b, N_HEADS, lq // tq),
        in_specs=specs,
        out_specs=pl.BlockSpec((None, tq, vd), lambda bb, h, i: (bb, i, h)),
        out_shape=jax.ShapeDtypeStruct((b, lq, d), BF16),
        compiler_params=_params(3),
        name="diff_flash",
    )(*ins)


def _out_proj_kernel(x_ref, o_ref_in, mod_ref, w_ref, out_ref, *, mod_base):
    gate = mod_ref[mod_base + 2:mod_base + 3, :]
    out_ref[...] = x_ref[...] + gate * _dot(o_ref_in[...], w_ref[...])


def _out_proj(x, o, mods, w, mod_base):
    b, n, d = x.shape
    tm = _token_tile(n)
    kern = functools.partial(_out_proj_kernel, mod_base=mod_base)
    return pl.pallas_call(
        kern,
        grid=(b, n // tm),
        in_specs=[_tok_spec(tm, d), _tok_spec(tm, d), _mod_spec(mods), _const_spec(w)],
        out_specs=_tok_spec(tm, d),
        out_shape=jax.ShapeDtypeStruct(x.shape, F32),
        compiler_params=_params(2),
        name="attn_out_proj",
    )(x, o, mods, w)


def _conf_in_kernel(x_ref, mod_ref, g_ref, w_ref, b_ref, u_ref, *, mod_base):
    x = x_ref[...]
    d = x.shape[1]
    shift, scale, _ = _mod_rows(mod_ref, mod_base)
    h = _ada_in(x, g_ref[...], shift, scale).astype(BF16)
    ag = _dot(h, w_ref[...]) + b_ref[...]
    u_ref[...] = ag[:, :d] * _sigmoid(ag[:, d:])


def _conf_in(x, mods, g, w_in, b_in, mod_base):
    b, n, d = x.shape
    tm = _token_tile(n)
    kern = functools.partial(_conf_in_kernel, mod_base=mod_base)
    return pl.pallas_call(
        kern,
        grid=(b, n // tm),
        in_specs=[_tok_spec(tm, d), _mod_spec(mods), _const_spec(g), _const_spec(w_in), _const_spec(b_in)],
        out_specs=_tok_spec(tm, d),
        out_shape=jax.ShapeDtypeStruct(x.shape, F32),
        compiler_params=_params(2),
        name="conf_in",
    )(x, mods, g, w_in, b_in)


def _sc_in_kernel(x_ref, mod_ref, g_ref, w_ref, b_out_ref, m_out_ref, *, mod_base):
    x = x_ref[...]
    d = x.shape[1]
    shift, scale, _ = _mod_rows(mod_ref, mod_base)
    h = _ada_in(x, g_ref[...], shift, scale).astype(BF16)
    y = _dot(h, w_ref[...])
    b_out_ref[...] = y[:, :d]
    m_out_ref[...] = y[:, d:2 * d] * y[:, 2 * d:]


def _sc_in(x, mods, g, w_in, mod_base):
    b, n, d = x.shape
    tm = _token_tile(n)
    kern = functools.partial(_sc_in_kernel, mod_base=mod_base)
    return pl.pallas_call(
        kern,
        grid=(b, n // tm),
        in_specs=[_tok_spec(tm, d), _mod_spec(mods), _const_spec(g), _const_spec(w_in)],
        out_specs=[_tok_spec(tm, d), _tok_spec(tm, d)],
        out_shape=[jax.ShapeDtypeStruct(x.shape, F32)] * 2,
        compiler_params=_params(2),
        name="sconv_in",
    )(x, mods, g, w_in)


def _dwconv(win_ref, cur_ref, prev_ref, next_ref, dw_ref, halo):
    tm = cur_ref.shape[0]
    taps = dw_ref.shape[0]
    i = pl.program_id(1)
    first = i == 0
    last = i == pl.num_programs(1) - 1
    win_ref[0:halo, :] = jnp.where(first, 0.0, prev_ref[...])
    win_ref[halo:halo + tm, :] = cur_ref[...]
    win_ref[halo + tm:, :] = jnp.where(last, 0.0, next_ref[...])
    off = halo - taps // 2
    acc = win_ref[off:off + tm, :] * dw_ref[0:1, :]
    for k in range(1, taps):
        acc = acc + win_ref[off + k:off + k + tm, :] * dw_ref[k:k + 1, :]
    return acc


def _conf_out_kernel(x_ref, mod_ref, cur_ref, prev_ref, next_ref, dw_ref, dwb_ref, lng_ref, lnb_ref,
                     w_ref, b_ref, o_ref, win_ref, *, mod_base, halo):
    u = _dwconv(win_ref, cur_ref, prev_ref, next_ref, dw_ref, halo) + dwb_ref[...]
    mu = jnp.mean(u, axis=-1, keepdims=True)
    uc = u - mu
    var = jnp.mean(uc * uc, axis=-1, keepdims=True)
    y = uc * lax.rsqrt(var + EPS) * lng_ref[...] + lnb_ref[...]
    y = (y * _sigmoid(y)).astype(BF16)
    gate = mod_ref[mod_base + 2:mod_base + 3, :]
    o_ref[...] = x_ref[...] + gate * (_dot(y, w_ref[...]) + b_ref[...])


def _sc_out_kernel(x_ref, mod_ref, bg_ref, cur_ref, prev_ref, next_ref, dw_ref, w_ref, o_ref, win_ref,
                   *, mod_base, halo):
    u = _dwconv(win_ref, cur_ref, prev_ref, next_ref, dw_ref, halo)
    y = (bg_ref[...] * u).astype(BF16)
    gate = mod_ref[mod_base + 2:mod_base + 3, :]
    o_ref[...] = x_ref[...] + gate * _dot(y, w_ref[...])


def _halo_specs(tm, d, halo, n):
    r = tm // halo
    nb = n // halo
    prev = pl.BlockSpec((None, halo, d), lambda b, i: (b, jnp.maximum(i * r - 1, 0), 0))
    nxt = pl.BlockSpec((None, halo, d), lambda b, i: (b, jnp.minimum((i + 1) * r, nb - 1), 0))
    return prev, nxt


def _conf_out(x, u, mods, dw_w, dw_b, ln_g, ln_b, w_out, b_out, mod_base):
    b, n, d = x.shape
    tm = _token_tile(n)
    halo = 16
    prev, nxt = _halo_specs(tm, d, halo, n)
    kern = functools.partial(_conf_out_kernel, mod_base=mod_base, halo=halo)
    consts = [dw_w, dw_b, ln_g, ln_b, w_out, b_out]
    return pl.pallas_call(
        kern,
        grid=(b, n // tm),
        in_specs=[_tok_spec(tm, d), _mod_spec(mods), _tok_spec(tm, d), prev, nxt] + [_const_spec(a) for a in consts],
        out_specs=_tok_spec(tm, d),
        out_shape=jax.ShapeDtypeStruct(x.shape, F32),
        scratch_shapes=[pltpu.VMEM((tm + 2 * halo, d), F32)],
        compiler_params=_params(2),
        name="conf_out",
    )(x, mods, u, u, u, *consts)


def _sc_out(x, bg, m, mods, dw_w, w_out, mod_base):
    b, n, d = x.shape
    tm = _token_tile(n)
    halo = 8
    prev, nxt = _halo_specs(tm, d, halo, n)
    kern = functools.partial(_sc_out_kernel, mod_base=mod_base, halo=halo)
    return pl.pallas_call(
        kern,
        grid=(b, n // tm),
        in_specs=[_tok_spec(tm, d), _mod_spec(mods), _tok_spec(tm, d), _tok_spec(tm, d), prev, nxt,
                  _const_spec(dw_w), _const_spec(w_out)],
        out_specs=_tok_spec(tm, d),
        out_shape=jax.ShapeDtypeStruct(x.shape, F32),
        scratch_shapes=[pltpu.VMEM((tm + 2 * halo, d), F32)],
        compiler_params=_params(2),
        name="sconv_out",
    )(x, mods, bg, m, m, m, dw_w, w_out)


def _rope_tables(n_tokens, head_dim):
    rows = n_tokens // GRID_W
    row_ids = jnp.repeat(jnp.arange(rows, dtype=F32), GRID_W)
    col_ids = jnp.tile(jnp.arange(GRID_W, dtype=F32), rows)
    half = head_dim // 2
    inv_freq = ROPE_THETA ** (-jnp.arange(0, half, 2, dtype=F32) / half)
    ang_r = row_ids[:, None] * inv_freq
    ang_c = col_ids[:, None] * inv_freq
    ang = jnp.concatenate([ang_r, ang_r, ang_c, ang_c], axis=-1)
    quarter = head_dim // 4
    sign = jnp.where((jnp.arange(head_dim) % (2 * quarter)) < quarter, -1.0, 1.0).astype(F32)
    reps = LANES // head_dim
    return jnp.tile(jnp.cos(ang), (1, reps)), jnp.tile(jnp.sin(ang) * sign, (1, reps))


def _group_mean_matrix(head_dim):
    idx = jnp.arange(MXU_DIM) // head_dim
    return jnp.where(idx[:, None] == idx[None, :], 1.0 / head_dim, 0.0).astype(BF16)


def kernel(x, c, ctx, c_ctx, ada_w, ada_b, norm_g, ffn_w_in, ffn_w_out, attn_w_qkv, attn_w_o, attn_q_g,
           attn_k_g, attn_lambda, attn_subln_g, conv_w_in, conv_b_in, conv_dw_w, conv_dw_b, conv_ln_g,
           conv_ln_b, conv_w_out, conv_b_out, sc_w_in, sc_dw_w, sc_w_out):
    bsz, seq, d = x.shape
    depth = ada_w.shape[0]
    head_dim = d // (2 * N_HEADS)
    row = lambda v: v.reshape(1, -1)

    n_rows = ((bsz + 1 + 7) // 8) * 8
    cvec = jnp.zeros((n_rows, d), F32).at[:bsz].set(c).at[bsz].set(c_ctx)
    mods = _modulations(cvec, ada_w, ada_b).reshape(depth, n_rows, N_MOD, d)

    rope = _rope_tables(seq, head_dim)
    gmat = _group_mean_matrix(head_dim)

    for i in range(depth):
        kind = i % N_MIXERS
        j = i // N_MIXERS
        last = i == depth - 1
        ctx_in = (not last) or kind == 0
        ctx_out = not last
        ml = mods[i, :bsz]
        mc = mods[i, bsz:bsz + 1]
        g = [row(norm_g[i, s]) for s in range(3)]
        w_in = [ffn_w_in[i, s].astype(BF16) for s in range(2)]
        w_out = [ffn_w_out[i, s].astype(BF16) for s in range(2)]

        x = _ffn(x, ml, g[0], w_in[0], w_out[0], 0)
        if ctx_in:
            ctx = _ffn(ctx, mc, g[0], w_in[0], w_out[0], 0)

        if kind == 0:
            lam_init = 0.8 - 0.6 * math.exp(-0.3 * i)
            w_qkv = attn_w_qkv[j].astype(BF16)
            w_o = attn_w_o[j].astype(BF16)
            qg = jnp.tile(attn_q_g[j], MXU_DIM // head_dim).reshape(1, MXU_DIM)
            kg = jnp.tile(attn_k_g[j], MXU_DIM // head_dim).reshape(1, MXU_DIM)
            subg = attn_subln_g[j].reshape(-1, 1)
            q_l, k_l, vt_l = _qkv_proj(x, ml, g[1], w_qkv, gmat, qg, kg, rope, 3)
            q_c, k_c, vt_c = _qkv_proj(ctx, mc, g[1], w_qkv, gmat, qg, kg, None, 3)
            o_l = _flash(q_l, [(k_l, vt_l), (k_c, vt_c)], attn_lambda[j], subg, lam_init, 256, 512)
            x = _out_proj(x, o_l, ml, w_o, 3)
            if ctx_out:
                o_c = _flash(q_c, [(k_c, vt_c)], attn_lambda[j], subg, lam_init, 256, 512)
                ctx = _out_proj(ctx, o_c, mc, w_o, 3)
        elif kind == 1:
            cw_in = conv_w_in[j].astype(BF16)
            cw_out = conv_w_out[j].astype(BF16)
            cargs = (conv_dw_w[j], row(conv_dw_b[j]), row(conv_ln_g[j]), row(conv_ln_b[j]), cw_out,
                     row(conv_b_out[j]))
            u = _conf_in(x, ml, g[1], cw_in, row(conv_b_in[j]), 3)
            x = _conf_out(x, u, ml, *cargs, 3)
            if ctx_out:
                u = _conf_in(ctx, mc, g[1], cw_in, row(conv_b_in[j]), 3)
                ctx = _conf_out(ctx, u, mc, *cargs, 3)
        else:
            sw_in = sc_w_in[j].astype(BF16)
            sw_out = sc_w_out[j].astype(BF16)
            bg, m = _sc_in(x, ml, g[1], sw_in, 3)
            x = _sc_out(x, bg, m, ml, sc_dw_w[j], sw_out, 3)
            if ctx_out:
                bg, m = _sc_in(ctx, mc, g[1], sw_in, 3)
                ctx = _sc_out(ctx, bg, m, mc, sc_dw_w[j], sw_out, 3)

        x = _ffn(x, ml, g[2], w_in[1], w_out[1], 6)
        if ctx_out:
            ctx = _ffn(ctx, mc, g[2], w_in[1], w_out[1], 6)

    return x
```

```python
import functools
import math

import jax
import jax.numpy as jnp
from jax import lax
from jax.experimental import pallas as pl
from jax.experimental.pallas import tpu as pltpu

GRID_W = 64
N_MIXERS = 3
N_HEADS = 8
N_MOD = 9
ROPE_THETA = 10000.0
EPS = 1e-6

LANES = 128
SUBLANES = 8
MXU_DIM = 256
VMEM_LIMIT = 56 * 1024 * 1024
FLASH_TQ = 512
FLASH_TK = 1024
MAX_FIXED_SHIFT = 60.0
LOG2E = math.log2(math.e)

BF16 = jnp.bfloat16
F32 = jnp.float32


def _dot(a, b):
    return jnp.dot(a, b, preferred_element_type=F32)


def _dot_nt(a, b):
    return lax.dot_general(a, b, (((1,), (1,)), ((), ())), preferred_element_type=F32)


def _sigmoid(x):
    return 1.0 / (1.0 + jnp.exp(-x))


def _ada_in(x, g, shift, scale):
    y = x * lax.rsqrt(jnp.mean(x * x, axis=-1, keepdims=True) + EPS)
    return (y * g) * (1.0 + scale) + shift


def _mod_rows(mod_ref, base):
    return (mod_ref[base:base + 1, :], mod_ref[base + 1:base + 2, :], mod_ref[base + 2:base + 3, :])


def _tok_spec(tm, d):
    return pl.BlockSpec((None, tm, d), lambda b, i: (b, i, 0))


def _mod_spec(mods):
    per_batch = mods.shape[0] > 1
    return pl.BlockSpec((None,) + mods.shape[1:], lambda b, i: (b if per_batch else 0, 0, 0))


def _const_spec(arr):
    nd = arr.ndim
    return pl.BlockSpec(arr.shape, lambda *_: (0,) * nd, pipeline_mode=pl.Buffered(1))


def _params(n_grid):
    return pltpu.CompilerParams(dimension_semantics=("parallel",) * n_grid,
                                vmem_limit_bytes=VMEM_LIMIT)


def _token_tile(n):
    return 512 if n % 512 == 0 else n


def _mods_kernel(c_ref, w_ref, b_ref, o_ref):
    c = c_ref[...]
    sc = (c * _sigmoid(c)).astype(BF16)
    o_ref[...] = _dot(sc, w_ref[...].astype(BF16)) + b_ref[...]


def _modulations(cvec, ada_w, ada_b):
    depth, d, n = ada_w.shape
    r = cvec.shape[0]
    tn = n // 8
    return pl.pallas_call(
        _mods_kernel,
        grid=(depth, n // tn),
        in_specs=[pl.BlockSpec((r, d), lambda l, j: (0, 0)),
                  pl.BlockSpec((None, d, tn), lambda l, j: (l, 0, j)),
                  pl.BlockSpec((None, 1, tn), lambda l, j: (l, 0, j))],
        out_specs=pl.BlockSpec((None, r, tn), lambda l, j: (l, 0, j)),
        out_shape=jax.ShapeDtypeStruct((depth, r, n), F32),
        compiler_params=_params(2),
        name="adaln_mods",
    )(cvec, ada_w, ada_b.reshape(depth, 1, n))


def _ffn_kernel(x_ref, mod_ref, g_ref, win_ref, wout_ref, o_ref, u_ref, *, mod_base, fc):
    x = x_ref[...]
    shift, scale, gate = _mod_rows(mod_ref, mod_base)
    h = _ada_in(x, g_ref[...], shift, scale).astype(BF16)
    f = wout_ref.shape[0]
    for c in range(f // fc):
        a = _dot(h, win_ref[:, c * fc:(c + 1) * fc])
        gt = _dot(h, win_ref[:, f + c * fc:f + (c + 1) * fc])
        u_ref[:, c * fc:(c + 1) * fc] = ((gt * _sigmoid(gt)) * a).astype(BF16)
    y = _dot(u_ref[...], wout_ref[...])
    o_ref[...] = x + (0.5 * gate) * y


def _ffn(x, mods, g, w_in, w_out, mod_base):
    b, n, d = x.shape
    f = w_out.shape[0]
    tm = _token_tile(n)
    kern = functools.partial(_ffn_kernel, mod_base=mod_base, fc=MXU_DIM)
    return pl.pallas_call(
        kern,
        grid=(b, n // tm),
        in_specs=[_tok_spec(tm, d), _mod_spec(mods), _const_spec(g), _const_spec(w_in), _const_spec(w_out)],
        out_specs=_tok_spec(tm, d),
        out_shape=jax.ShapeDtypeStruct(x.shape, F32),
        scratch_shapes=[pltpu.VMEM((tm, f), BF16)],
        compiler_params=_params(2),
        name="ffn",
    )(x, mods, g, w_in, w_out)


def _qkv_kernel(*refs, mod_base, use_rope):
    if use_rope:
        (x_ref, mod_ref, g_ref, w_ref, gm_ref, qg_ref, kg_ref, cos_ref, sin_ref,
         q_ref, k_ref, vt_ref) = refs
    else:
        x_ref, mod_ref, g_ref, w_ref, gm_ref, qg_ref, kg_ref, q_ref, k_ref, vt_ref = refs
    x = x_ref[...]
    d = x.shape[1]
    shift, scale, _ = _mod_rows(mod_ref, mod_base)
    h = _ada_in(x, g_ref[...], shift, scale).astype(BF16)
    qkv = _dot(h, w_ref[...])

    if use_rope:
        cos = cos_ref[...]
        sin = sin_ref[...]
        lane = lax.broadcasted_iota(jnp.int32, cos.shape, 1)
        low_half = (lane % 32) < 16

    def norm_rope(t, gain_ref, out_ref, post_scale):
        for p in range(d // MXU_DIM):
            sl = t[:, p * MXU_DIM:(p + 1) * MXU_DIM]
            ms = _dot((sl * sl).astype(BF16), gm_ref[...])
            sl = sl * lax.rsqrt(ms + EPS) * gain_ref[...]
            for hh in range(MXU_DIM // LANES):
                y = sl[:, hh * LANES:(hh + 1) * LANES]
                if use_rope:
                    rot = jnp.where(low_half, pltpu.roll(y, LANES - 16, 1), pltpu.roll(y, 16, 1))
                    y = y * cos + rot * sin
                c0 = p * MXU_DIM + hh * LANES
                out_ref[:, c0:c0 + LANES] = (y * post_scale).astype(BF16)

    head_dim = d // (2 * N_HEADS)
    norm_rope(qkv[:, :d], qg_ref, q_ref, head_dim ** -0.5 * LOG2E)
    norm_rope(qkv[:, d:2 * d], kg_ref, k_ref, 1.0)
    vt_ref[...] = qkv[:, 2 * d:].T.astype(BF16)


def _qkv_proj(x, mods, g, w_qkv, gmat, qg, kg, rope, mod_base):
    b, n, d = x.shape
    tm = _token_tile(n)
    use_rope = rope is not None
    ins = [x, mods, g, w_qkv, gmat, qg, kg]
    specs = [_tok_spec(tm, d), _mod_spec(mods), _const_spec(g), _const_spec(w_qkv), _const_spec(gmat),
             _const_spec(qg), _const_spec(kg)]
    if use_rope:
        ins += list(rope)
        specs += [pl.BlockSpec((tm, LANES), lambda bb, i: (i, 0))] * 2
    kern = functools.partial(_qkv_kernel, mod_base=mod_base, use_rope=use_rope)
    return pl.pallas_call(
        kern,
        grid=(b, n // tm),
        in_specs=specs,
        out_specs=[_tok_spec(tm, d), _tok_spec(tm, d),
                   pl.BlockSpec((None, d, tm), lambda bb, i: (bb, 0, i))],
        out_shape=[jax.ShapeDtypeStruct((b, n, d), BF16), jax.ShapeDtypeStruct((b, n, d), BF16),
                   jax.ShapeDtypeStruct((b, d, n), BF16)],
        compiler_params=_params(2),
        name="qkv_proj",
    )(*ins)


def _flash_kernel(*refs, n_seg, tk, lam_init, online):
    shift_ref, lamp_ref, subg_ref, q_ref = refs[:4]
    seg_refs = refs[4:4 + 2 * n_seg]
    o_ref = refs[4 + 2 * n_seg]

    lp = lamp_ref[...]
    lam = (jnp.exp(jnp.sum(lp[0:1] * lp[1:2], keepdims=True))
           - jnp.exp(jnp.sum(lp[2:3] * lp[3:4], keepdims=True)) + lam_init)

    q = q_ref[...]
    tq, vd = q.shape
    lane = lax.broadcasted_iota(jnp.int32, q.shape, 1)
    zero = jnp.zeros_like(q)
    qm = (jnp.where(lane < vd // 2, q, zero), jnp.where(lane >= vd // 2, q, zero))

    if online:
        m = [jnp.full((1, tq), -jnp.inf, F32) for _ in range(2)]
        l = [jnp.zeros((1, tq), F32) for _ in range(2)]
        acc = [jnp.zeros((vd, tq), F32) for _ in range(2)]
    else:
        shift = shift_ref[0]
        q2 = jnp.concatenate(qm, axis=0)
        l2 = jnp.zeros((8, 2 * tq), F32)
        acc2 = jnp.zeros((vd, 2 * tq), F32)
    for s in range(n_seg):
        k_ref, vt_ref = seg_refs[2 * s], seg_refs[2 * s + 1]
        nk = k_ref.shape[0]
        ck = min(tk, nk)
        for c in range(nk // ck):
            kch = k_ref[c * ck:(c + 1) * ck, :]
            vtch = vt_ref[:, c * ck:(c + 1) * ck]
            if online:
                for j in range(2):
                    st = _dot_nt(kch, qm[j])
                    m_new = jnp.maximum(m[j], jnp.max(st, axis=0, keepdims=True))
                    alpha = jnp.exp2(m[j] - m_new)
                    p = jnp.exp2(st - m_new)
                    l[j] = alpha * l[j] + jnp.sum(p, axis=0, keepdims=True)
                    acc[j] = alpha * acc[j] + _dot(vtch, p.astype(BF16))
                    m[j] = m_new
            else:
                p = jnp.exp2(_dot_nt(kch, q2) - shift)
                l2 = l2 + jnp.sum(p.reshape(ck // 8, 8, 2 * tq), axis=0)
                acc2 = acc2 + _dot(vtch, p.astype(BF16))

    if not online:
        l2 = jnp.sum(l2, axis=0, keepdims=True)
        l = [l2[:, :tq], l2[:, tq:]]
        acc = [acc2[:, :tq], acc2[:, tq:]]
    ot = acc[0] / l[0] - lam * (acc[1] / l[1])
    ot = ot * lax.rsqrt(jnp.mean(ot * ot, axis=0, keepdims=True) + EPS)
    ot = ot * subg_ref[...] * (1.0 - lam_init)
    o_ref[...] = ot.T.astype(BF16)


def _flash(q, segs, lam_p, subg, score_bound, lam_init):
    b, lq, d = q.shape
    vd = d // N_HEADS
    tq = min(FLASH_TQ, lq)
    ins = [score_bound.reshape(1), lam_p, subg, q]
    specs = [pl.BlockSpec(memory_space=pltpu.SMEM),
             pl.BlockSpec(lam_p.shape, lambda bb, h, i: (0, 0)),
             pl.BlockSpec(subg.shape, lambda bb, h, i: (0, 0)),
             pl.BlockSpec((None, tq, vd), lambda bb, h, i: (bb, i, h))]
    for k, vt in segs:
        nk = k.shape[1]
        ins += [k, vt]
        specs += [pl.BlockSpec((None, nk, vd), lambda bb, h, i: (bb, 0, h)),
                  pl.BlockSpec((None, vd, nk), lambda bb, h, i: (bb, h, 0))]

    def call(online):
        kern = functools.partial(_flash_kernel, n_seg=len(segs), tk=FLASH_TK, lam_init=lam_init, online=online)
        return pl.pallas_call(
            kern,
            grid=(b, N_HEADS, lq // tq),
            in_specs=specs,
            out_specs=pl.BlockSpec((None, tq, vd), lambda bb, h, i: (bb, i, h)),
            out_shape=jax.ShapeDtypeStruct((b, lq, d), BF16),
            compiler_params=_params(3),
            name="diff_flash_online" if online else "diff_flash",
        )

    return lax.cond(score_bound <= MAX_FIXED_SHIFT, lambda *a: call(False)(*a), lambda *a: call(True)(*a), *ins)


def _out_proj_kernel(x_ref, o_ref_in, mod_ref, w_ref, out_ref, *, mod_base):
    gate = mod_ref[mod_base + 2:mod_base + 3, :]
    out_ref[...] = x_ref[...] + gate * _dot(o_ref_in[...], w_ref[...])


def _out_proj(x, o, mods, w, mod_base):
    b, n, d = x.shape
    tm = _token_tile(n)
    kern = functools.partial(_out_proj_kernel, mod_base=mod_base)
    return pl.pallas_call(
        kern,
        grid=(b, n // tm),
        in_specs=[_tok_spec(tm, d), _tok_spec(tm, d), _mod_spec(mods), _const_spec(w)],
        out_specs=_tok_spec(tm, d),
        out_shape=jax.ShapeDtypeStruct(x.shape, F32),
        compiler_params=_params(2),
        name="attn_out_proj",
    )(x, o, mods, w)


def _conf_in_kernel(x_ref, mod_ref, g_ref, w_ref, b_ref, u_ref, *, mod_base):
    x = x_ref[...]
    d = x.shape[1]
    shift, scale, _ = _mod_rows(mod_ref, mod_base)
    h = _ada_in(x, g_ref[...], shift, scale).astype(BF16)
    ag = _dot(h, w_ref[...]) + b_ref[...]
    u_ref[...] = ag[:, :d] * _sigmoid(ag[:, d:])


def _conf_in(x, mods, g, w_in, b_in, mod_base):
    b, n, d = x.shape
    tm = _token_tile(n)
    kern = functools.partial(_conf_in_kernel, mod_base=mod_base)
    return pl.pallas_call(
        kern,
        grid=(b, n // tm),
        in_specs=[_tok_spec(tm, d), _mod_spec(mods), _const_spec(g), _const_spec(w_in), _const_spec(b_in)],
        out_specs=_tok_spec(tm, d),
        out_shape=jax.ShapeDtypeStruct(x.shape, F32),
        compiler_params=_params(2),
        name="conf_in",
    )(x, mods, g, w_in, b_in)


def _sc_in_kernel(x_ref, mod_ref, g_ref, w_ref, b_out_ref, m_out_ref, *, mod_base):
    x = x_ref[...]
    d = x.shape[1]
    shift, scale, _ = _mod_rows(mod_ref, mod_base)
    h = _ada_in(x, g_ref[...], shift, scale).astype(BF16)
    y = _dot(h, w_ref[...])
    b_out_ref[...] = y[:, :d]
    m_out_ref[...] = y[:, d:2 * d] * y[:, 2 * d:]


def _sc_in(x, mods, g, w_in, mod_base):
    b, n, d = x.shape
    tm = _token_tile(n)
    kern = functools.partial(_sc_in_kernel, mod_base=mod_base)
    return pl.pallas_call(
        kern,
        grid=(b, n // tm),
        in_specs=[_tok_spec(tm, d), _mod_spec(mods), _const_spec(g), _const_spec(w_in)],
        out_specs=[_tok_spec(tm, d), _tok_spec(tm, d)],
        out_shape=[jax.ShapeDtypeStruct(x.shape, F32)] * 2,
        compiler_params=_params(2),
        name="sconv_in",
    )(x, mods, g, w_in)


def _dwconv(win_ref, al_ref, cur_ref, prev_ref, next_ref, dw_ref, halo):
    tm = cur_ref.shape[0]
    taps = dw_ref.shape[0]
    i = pl.program_id(1)
    first = i == 0
    last = i == pl.num_programs(1) - 1
    win_ref[0:halo, :] = jnp.where(first, 0.0, prev_ref[...])
    win_ref[halo:halo + tm, :] = cur_ref[...]
    win_ref[halo + tm:, :] = jnp.where(last, 0.0, next_ref[...])
    off = halo - taps // 2
    acc = None
    for s in range(min(SUBLANES, taps)):
        ks = range(s, taps, SUBLANES)
        src, base = win_ref, off + s
        if len(ks) > 1 and base % SUBLANES:
            span = tm + SUBLANES * (len(ks) - 1)
            al_ref[0:span, :] = win_ref[base:base + span, :]
            src, base = al_ref, 0
        for jj, k in enumerate(ks):
            term = src[base + SUBLANES * jj:base + SUBLANES * jj + tm, :] * dw_ref[k:k + 1, :]
            acc = term if acc is None else acc + term
    return acc


def _conf_out_kernel(x_ref, mod_ref, cur_ref, prev_ref, next_ref, dw_ref, dwb_ref, lng_ref, lnb_ref,
                     w_ref, b_ref, o_ref, win_ref, al_ref, *, mod_base, halo):
    u = _dwconv(win_ref, al_ref, cur_ref, prev_ref, next_ref, dw_ref, halo) + dwb_ref[...]
    mu = jnp.mean(u, axis=-1, keepdims=True)
    uc = u - mu
    var = jnp.mean(uc * uc, axis=-1, keepdims=True)
    y = uc * lax.rsqrt(var + EPS) * lng_ref[...] + lnb_ref[...]
    y = (y * _sigmoid(y)).astype(BF16)
    gate = mod_ref[mod_base + 2:mod_base + 3, :]
    o_ref[...] = x_ref[...] + gate * (_dot(y, w_ref[...]) + b_ref[...])


def _sc_out_kernel(x_ref, mod_ref, bg_ref, cur_ref, prev_ref, next_ref, dw_ref, w_ref, o_ref, win_ref,
                   *, mod_base, halo):
    u = _dwconv(win_ref, None, cur_ref, prev_ref, next_ref, dw_ref, halo)
    y = (bg_ref[...] * u).astype(BF16)
    gate = mod_ref[mod_base + 2:mod_base + 3, :]
    o_ref[...] = x_ref[...] + gate * _dot(y, w_ref[...])


def _halo_specs(tm, d, halo, n):
    r = tm // halo
    nb = n // halo
    prev = pl.BlockSpec((None, halo, d), lambda b, i: (b, jnp.maximum(i * r - 1, 0), 0))
    nxt = pl.BlockSpec((None, halo, d), lambda b, i: (b, jnp.minimum((i + 1) * r, nb - 1), 0))
    return prev, nxt


def _conf_out(x, u, mods, dw_w, dw_b, ln_g, ln_b, w_out, b_out, mod_base):
    b, n, d = x.shape
    tm = _token_tile(n)
    halo = 16
    prev, nxt = _halo_specs(tm, d, halo, n)
    kern = functools.partial(_conf_out_kernel, mod_base=mod_base, halo=halo)
    consts = [dw_w, dw_b, ln_g, ln_b, w_out, b_out]
    return pl.pallas_call(
        kern,
        grid=(b, n // tm),
        in_specs=[_tok_spec(tm, d), _mod_spec(mods), _tok_spec(tm, d), prev, nxt] + [_const_spec(a) for a in consts],
        out_specs=_tok_spec(tm, d),
        out_shape=jax.ShapeDtypeStruct(x.shape, F32),
        scratch_shapes=[pltpu.VMEM((tm + 2 * halo, d), F32),
                        pltpu.VMEM((tm + SUBLANES * ((dw_w.shape[0] - 1) // SUBLANES), d), F32)],
        compiler_params=_params(2),
        name="conf_out",
    )(x, mods, u, u, u, *consts)


def _sc_out(x, bg, m, mods, dw_w, w_out, mod_base):
    b, n, d = x.shape
    tm = _token_tile(n)
    halo = 8
    prev, nxt = _halo_specs(tm, d, halo, n)
    kern = functools.partial(_sc_out_kernel, mod_base=mod_base, halo=halo)
    return pl.pallas_call(
        kern,
        grid=(b, n // tm),
        in_specs=[_tok_spec(tm, d), _mod_spec(mods), _tok_spec(tm, d), _tok_spec(tm, d), prev, nxt,
                  _const_spec(dw_w), _const_spec(w_out)],
        out_specs=_tok_spec(tm, d),
        out_shape=jax.ShapeDtypeStruct(x.shape, F32),
        scratch_shapes=[pltpu.VMEM((tm + 2 * halo, d), F32)],
        compiler_params=_params(2),
        name="sconv_out",
    )(x, mods, bg, m, m, m, dw_w, w_out)


def _rope_tables(n_tokens, head_dim):
    rows = n_tokens // GRID_W
    row_ids = jnp.repeat(jnp.arange(rows, dtype=F32), GRID_W)
    col_ids = jnp.tile(jnp.arange(GRID_W, dtype=F32), rows)
    half = head_dim // 2
    inv_freq = ROPE_THETA ** (-jnp.arange(0, half, 2, dtype=F32) / half)
    ang_r = row_ids[:, None] * inv_freq
    ang_c = col_ids[:, None] * inv_freq
    ang = jnp.concatenate([ang_r, ang_r, ang_c, ang_c], axis=-1)
    quarter = head_dim // 4
    sign = jnp.where((jnp.arange(head_dim) % (2 * quarter)) < quarter, -1.0, 1.0).astype(F32)
    reps = LANES // head_dim
    return jnp.tile(jnp.cos(ang), (1, reps)), jnp.tile(jnp.sin(ang) * sign, (1, reps))


def _group_mean_matrix(head_dim):
    idx = jnp.arange(MXU_DIM) // head_dim
    return jnp.where(idx[:, None] == idx[None, :], 1.0 / head_dim, 0.0).astype(BF16)


def kernel(x, c, ctx, c_ctx, ada_w, ada_b, norm_g, ffn_w_in, ffn_w_out, attn_w_qkv, attn_w_o, attn_q_g,
           attn_k_g, attn_lambda, attn_subln_g, conv_w_in, conv_b_in, conv_dw_w, conv_dw_b, conv_ln_g,
           conv_ln_b, conv_w_out, conv_b_out, sc_w_in, sc_dw_w, sc_w_out):
    bsz, seq, d = x.shape
    depth = ada_w.shape[0]
    head_dim = d // (2 * N_HEADS)
    row = lambda v: v.reshape(1, -1)

    n_rows = ((bsz + 1 + 7) // 8) * 8
    cvec = jnp.zeros((n_rows, d), F32).at[:bsz].set(c).at[bsz].set(c_ctx)
    mods = _modulations(cvec, ada_w, ada_b).reshape(depth, n_rows, N_MOD, d)

    rope = _rope_tables(seq, head_dim)
    gmat = _group_mean_matrix(head_dim)

    for i in range(depth):
        kind = i % N_MIXERS
        j = i // N_MIXERS
        last = i == depth - 1
        ctx_in = (not last) or kind == 0
        ctx_out = not last
        ml = mods[i, :bsz]
        mc = mods[i, bsz:bsz + 1]
        g = [row(norm_g[i, s]) for s in range(3)]
        w_in = [ffn_w_in[i, s].astype(BF16) for s in range(2)]
        w_out = [ffn_w_out[i, s].astype(BF16) for s in range(2)]

        x = _ffn(x, ml, g[0], w_in[0], w_out[0], 0)
        if ctx_in:
            ctx = _ffn(ctx, mc, g[0], w_in[0], w_out[0], 0)

        if kind == 0:
            lam_init = 0.8 - 0.6 * math.exp(-0.3 * i)
            w_qkv = attn_w_qkv[j].astype(BF16)
            w_o = attn_w_o[j].astype(BF16)
            qg = jnp.tile(attn_q_g[j], MXU_DIM // head_dim).reshape(1, MXU_DIM)
            kg = jnp.tile(attn_k_g[j], MXU_DIM // head_dim).reshape(1, MXU_DIM)
            subg = attn_subln_g[j].reshape(-1, 1)
            bound = (1.02 * LOG2E * head_dim ** 0.5) * jnp.max(jnp.abs(attn_q_g[j])) * jnp.max(jnp.abs(attn_k_g[j]))
            q_l, k_l, vt_l = _qkv_proj(x, ml, g[1], w_qkv, gmat, qg, kg, rope, 3)
            q_c, k_c, vt_c = _qkv_proj(ctx, mc, g[1], w_qkv, gmat, qg, kg, None, 3)
            o_l = _flash(q_l, [(k_l, vt_l), (k_c, vt_c)], attn_lambda[j], subg, bound, lam_init)
            x = _out_proj(x, o_l, ml, w_o, 3)
            if ctx_out:
                o_c = _flash(q_c, [(k_c, vt_c)], attn_lambda[j], subg, bound, lam_init)
                ctx = _out_proj(ctx, o_c, mc, w_o, 3)
        elif kind == 1:
            cw_in = conv_w_in[j].astype(BF16)
            cw_out = conv_w_out[j].astype(BF16)
            cargs = (conv_dw_w[j], row(conv_dw_b[j]), row(conv_ln_g[j]), row(conv_ln_b[j]), cw_out,
                     row(conv_b_out[j]))
            u = _conf_in(x, ml, g[1], cw_in, row(conv_b_in[j]), 3)
            x = _conf_out(x, u, ml, *cargs, 3)
            if ctx_out:
                u = _conf_in(ctx, mc, g[1], cw_in, row(conv_b_in[j]), 3)
                ctx = _conf_out(ctx, u, mc, *cargs, 3)
        else:
            sw_in = sc_w_in[j].astype(BF16)
            sw_out = sc_w_out[j].astype(BF16)
            bg, m = _sc_in(x, ml, g[1], sw_in, 3)
            x = _sc_out(x, bg, m, ml, sc_dw_w[j], sw_out, 3)
            if ctx_out:
                bg, m = _sc_in(ctx, mc, g[1], sw_in, 3)
                ctx = _sc_out(ctx, bg, m, mc, sc_dw_w[j], sw_out, 3)

        x = _ffn(x, ml, g[2], w_in[1], w_out[1], 6)
        if ctx_out:
            ctx = _ffn(ctx, mc, g[2], w_in[1], w_out[1], 6)

    return x
```

```python
import functools
import math

import jax
import jax.numpy as jnp
from jax import lax
from jax.experimental import pallas as pl
from jax.experimental.pallas import tpu as pltpu

GRID_W = 64
N_MIXERS = 3
N_HEADS = 8
N_MOD = 9
ROPE_THETA = 10000.0
EPS = 1e-6

LANES = 128
SUBLANES = 8
MXU_DIM = 256
VMEM_LIMIT = 56 * 1024 * 1024
TOKEN_TILE = 512
FFN_TILE = 1024
FLASH_TQ = 512
FLASH_TK = 4096
PROJ_CHUNK = 2 * MXU_DIM
MAX_FIXED_SHIFT = 60.0
LOG2E = math.log2(math.e)

BF16 = jnp.bfloat16
F32 = jnp.float32


def _dot(a, b):
    return jnp.dot(a, b, preferred_element_type=F32)


def _dot_nt(a, b):
    return lax.dot_general(a, b, (((1,), (1,)), ((), ())), preferred_element_type=F32)


def _sigmoid(x):
    return 1.0 / (1.0 + jnp.exp(-x))


def _ada_in(x, g, shift, scale):
    y = x * lax.rsqrt(jnp.mean(x * x, axis=-1, keepdims=True) + EPS)
    return (y * g) * (1.0 + scale) + shift


def _mod_rows(mod_ref, base):
    return (mod_ref[base:base + 1, :], mod_ref[base + 1:base + 2, :], mod_ref[base + 2:base + 3, :])


def _tok_spec(tm, d):
    return pl.BlockSpec((None, tm, d), lambda b, i: (b, i, 0))


def _mod_spec(mods):
    per_batch = mods.shape[0] > 1
    return pl.BlockSpec((None,) + mods.shape[1:], lambda b, i: (b if per_batch else 0, 0, 0))


def _const_spec(arr):
    nd = arr.ndim
    return pl.BlockSpec(arr.shape, lambda *_: (0,) * nd, pipeline_mode=pl.Buffered(1))


def _params(n_grid):
    return pltpu.CompilerParams(dimension_semantics=("parallel",) * n_grid,
                                vmem_limit_bytes=VMEM_LIMIT)


def _token_tile(n):
    return TOKEN_TILE if n % TOKEN_TILE == 0 else n


def _ffn_tile(n):
    return FFN_TILE if n % FFN_TILE == 0 else n


def _mods_kernel(c_ref, w_ref, b_ref, o_ref):
    c = c_ref[...]
    sc = (c * _sigmoid(c)).astype(BF16)
    o_ref[...] = _dot(sc, w_ref[...].astype(BF16)) + b_ref[...]


def _modulations(cvec, ada_w, ada_b):
    depth, d, n = ada_w.shape
    r = cvec.shape[0]
    tn = n // 8
    return pl.pallas_call(
        _mods_kernel,
        grid=(depth, n // tn),
        in_specs=[pl.BlockSpec((r, d), lambda l, j: (0, 0)),
                  pl.BlockSpec((None, d, tn), lambda l, j: (l, 0, j)),
                  pl.BlockSpec((None, 1, tn), lambda l, j: (l, 0, j))],
        out_specs=pl.BlockSpec((None, r, tn), lambda l, j: (l, 0, j)),
        out_shape=jax.ShapeDtypeStruct((depth, r, n), F32),
        compiler_params=_params(2),
        name="adaln_mods",
    )(cvec, ada_w, ada_b.reshape(depth, 1, n))


def _swiglu_half_step(x, mod_ref, mod_base, g_ref, win_ref, wout_ref, u_ref):
    shift, scale, gate = _mod_rows(mod_ref, mod_base)
    h = _ada_in(x, g_ref[...], shift, scale).astype(BF16)
    f = wout_ref.shape[0]
    for c in range(0, f, MXU_DIM):
        a = _dot(h, win_ref[:, c:c + MXU_DIM])
        gt = _dot(h, win_ref[:, f + c:f + c + MXU_DIM])
        u_ref[:, c:c + MXU_DIM] = ((gt * _sigmoid(gt)) * a).astype(BF16)
    return x + (0.5 * gate) * _dot(u_ref[...], wout_ref[...])


def _ffn_kernel(x_ref, mod_ref, g_ref, win_ref, wout_ref, o_ref, u_ref, *, mod_base):
    o_ref[...] = _swiglu_half_step(x_ref[...], mod_ref, mod_base, g_ref, win_ref, wout_ref, u_ref)


def _attn_out_ffn_kernel(x_ref, a_ref, mod_ref, wo_ref, g_ref, win_ref, wout_ref, o_ref, u_ref, *, mod_base):
    gate = mod_ref[mod_base + 2:mod_base + 3, :]
    x = x_ref[...] + gate * _dot(a_ref[...], wo_ref[...])
    o_ref[...] = _swiglu_half_step(x, mod_ref, mod_base + 3, g_ref, win_ref, wout_ref, u_ref)


def _ffn(x, mods, g, w_in, w_out, mod_base):
    b, n, d = x.shape
    f = w_out.shape[0]
    assert f % MXU_DIM == 0
    tm = _ffn_tile(n)
    kern = functools.partial(_ffn_kernel, mod_base=mod_base)
    return pl.pallas_call(
        kern,
        grid=(b, n // tm),
        in_specs=[_tok_spec(tm, d), _mod_spec(mods), _const_spec(g), _const_spec(w_in), _const_spec(w_out)],
        out_specs=_tok_spec(tm, d),
        out_shape=jax.ShapeDtypeStruct(x.shape, F32),
        scratch_shapes=[pltpu.VMEM((tm, f), BF16)],
        compiler_params=_params(2),
        name="ffn",
    )(x, mods, g, w_in, w_out)


def _attn_out_ffn(x, attn, mods, w_o, g, w_in, w_out, mod_base):
    b, n, d = x.shape
    f = w_out.shape[0]
    assert f % MXU_DIM == 0
    tm = _ffn_tile(n)
    kern = functools.partial(_attn_out_ffn_kernel, mod_base=mod_base)
    return pl.pallas_call(
        kern,
        grid=(b, n // tm),
        in_specs=[_tok_spec(tm, d), _tok_spec(tm, d), _mod_spec(mods), _const_spec(w_o), _const_spec(g),
                  _const_spec(w_in), _const_spec(w_out)],
        out_specs=_tok_spec(tm, d),
        out_shape=jax.ShapeDtypeStruct(x.shape, F32),
        scratch_shapes=[pltpu.VMEM((tm, f), BF16)],
        compiler_params=_params(2),
        name="attn_out_ffn",
    )(x, attn, mods, w_o, g, w_in, w_out)


def _qkv_kernel(*refs, mod_base, use_rope):
    if use_rope:
        (x_ref, mod_ref, g_ref, w_ref, gm_ref, qg_ref, kg_ref, cos_ref, sin_ref,
         q_ref, k_ref, vt_ref) = refs
    else:
        x_ref, mod_ref, g_ref, w_ref, gm_ref, qg_ref, kg_ref, q_ref, k_ref, vt_ref = refs
    x = x_ref[...]
    d = x.shape[1]
    shift, scale, _ = _mod_rows(mod_ref, mod_base)
    h = _ada_in(x, g_ref[...], shift, scale).astype(BF16)
    if use_rope:
        cos = cos_ref[...]
        sin = sin_ref[...]
        lane = lax.broadcasted_iota(jnp.int32, cos.shape, 1)
        low_half = (lane % 32) < 16

    def norm_rope(col0, gain_ref, out_ref, post_scale):
        for cc in range(0, d, PROJ_CHUNK):
            t = _dot(h, w_ref[:, col0 + cc:col0 + cc + PROJ_CHUNK])
            for p in range(0, PROJ_CHUNK, MXU_DIM):
                sl = t[:, p:p + MXU_DIM]
                ms = _dot((sl * sl).astype(BF16), gm_ref[...])
                sl = sl * lax.rsqrt(ms + EPS) * gain_ref[...]
                for hh in range(0, MXU_DIM, LANES):
                    y = sl[:, hh:hh + LANES]
                    if use_rope:
                        rot = jnp.where(low_half, pltpu.roll(y, LANES - 16, 1), pltpu.roll(y, 16, 1))
                        y = y * cos + rot * sin
                    c0 = cc + p + hh
                    out_ref[:, c0:c0 + LANES] = (y * post_scale).astype(BF16)

    head_dim = d // (2 * N_HEADS)
    norm_rope(0, qg_ref, q_ref, head_dim ** -0.5 * LOG2E)
    norm_rope(d, kg_ref, k_ref, 1.0)
    for cc in range(0, d, PROJ_CHUNK):
        v = _dot(h, w_ref[:, 2 * d + cc:2 * d + cc + PROJ_CHUNK])
        vt_ref[cc:cc + PROJ_CHUNK, :] = v.T.astype(BF16)


def _qkv_proj(x, mods, g, w_qkv, gmat, qg, kg, rope, mod_base):
    b, n, d = x.shape
    tm = _token_tile(n)
    use_rope = rope is not None
    ins = [x, mods, g, w_qkv, gmat, qg, kg]
    specs = [_tok_spec(tm, d), _mod_spec(mods), _const_spec(g), _const_spec(w_qkv), _const_spec(gmat),
             _const_spec(qg), _const_spec(kg)]
    if use_rope:
        ins += list(rope)
        specs += [pl.BlockSpec((tm, LANES), lambda bb, i: (i, 0))] * 2
    kern = functools.partial(_qkv_kernel, mod_base=mod_base, use_rope=use_rope)
    return pl.pallas_call(
        kern,
        grid=(b, n // tm),
        in_specs=specs,
        out_specs=[_tok_spec(tm, d), _tok_spec(tm, d),
                   pl.BlockSpec((None, d, tm), lambda bb, i: (bb, 0, i))],
        out_shape=[jax.ShapeDtypeStruct((b, n, d), BF16), jax.ShapeDtypeStruct((b, n, d), BF16),
                   jax.ShapeDtypeStruct((b, d, n), BF16)],
        compiler_params=_params(2),
        name="qkv_proj",
    )(*ins)


def _flash_kernel(*refs, n_seg, tk, lam_init, online):
    shift_ref, lamp_ref, subg_ref, q_ref = refs[:4]
    seg_refs = refs[4:4 + 2 * n_seg]
    o_ref = refs[4 + 2 * n_seg]

    lp = lamp_ref[...]
    lam = (jnp.exp(jnp.sum(lp[0:1] * lp[1:2], keepdims=True))
           - jnp.exp(jnp.sum(lp[2:3] * lp[3:4], keepdims=True)) + lam_init)

    q = q_ref[...]
    tq, vd = q.shape
    lane = lax.broadcasted_iota(jnp.int32, q.shape, 1)
    zero = jnp.zeros_like(q)
    qm = (jnp.where(lane < vd // 2, q, zero), jnp.where(lane >= vd // 2, q, zero))

    if online:
        m = [jnp.full((1, tq), -jnp.inf, F32) for _ in range(2)]
        l = [jnp.zeros((1, tq), F32) for _ in range(2)]
        acc = [jnp.zeros((vd, tq), F32) for _ in range(2)]
    else:
        shift = shift_ref[0]
        q2 = jnp.concatenate(qm, axis=0)
        l2 = jnp.zeros((8, 2 * tq), F32)
        acc2 = jnp.zeros((vd, 2 * tq), F32)
    for s in range(n_seg):
        k_ref, vt_ref = seg_refs[2 * s], seg_refs[2 * s + 1]
        nk = k_ref.shape[0]
        ck = min(tk, nk)
        for c in range(nk // ck):
            kch = k_ref[c * ck:(c + 1) * ck, :]
            vtch = vt_ref[:, c * ck:(c + 1) * ck]
            if online:
                for j in range(2):
                    st = _dot_nt(kch, qm[j])
                    m_new = jnp.maximum(m[j], jnp.max(st, axis=0, keepdims=True))
                    alpha = jnp.exp2(m[j] - m_new)
                    p = jnp.exp2(st - m_new)
                    l[j] = alpha * l[j] + jnp.sum(p, axis=0, keepdims=True)
                    acc[j] = alpha * acc[j] + _dot(vtch, p.astype(BF16))
                    m[j] = m_new
            else:
                p = jnp.exp2(_dot_nt(kch, q2) - shift)
                l2 = l2 + jnp.sum(p.reshape(ck // 8, 8, 2 * tq), axis=0)
                acc2 = acc2 + _dot(vtch, p.astype(BF16))

    if not online:
        l2 = jnp.sum(l2, axis=0, keepdims=True)
        l = [l2[:, :tq], l2[:, tq:]]
        acc = [acc2[:, :tq], acc2[:, tq:]]
    ot = acc[0] / l[0] - lam * (acc[1] / l[1])
    ot = ot * lax.rsqrt(jnp.mean(ot * ot, axis=0, keepdims=True) + EPS)
    ot = ot * subg_ref[...] * (1.0 - lam_init)
    o_ref[...] = ot.T.astype(BF16)


def _flash(q, segs, lam_p, subg, score_bound, lam_init):
    b, lq, d = q.shape
    vd = d // N_HEADS
    tq = min(FLASH_TQ, lq)
    ins = [score_bound.reshape(1), lam_p, subg, q]
    specs = [pl.BlockSpec(memory_space=pltpu.SMEM),
             pl.BlockSpec(lam_p.shape, lambda bb, h, i: (0, 0)),
             pl.BlockSpec(subg.shape, lambda bb, h, i: (0, 0)),
             pl.BlockSpec((None, tq, vd), lambda bb, h, i: (bb, i, h))]
    for k, vt in segs:
        nk = k.shape[1]
        ins += [k, vt]
        specs += [pl.BlockSpec((None, nk, vd), lambda bb, h, i: (bb, 0, h)),
                  pl.BlockSpec((None, vd, nk), lambda bb, h, i: (bb, h, 0))]

    def call(online):
        kern = functools.partial(_flash_kernel, n_seg=len(segs), tk=FLASH_TK, lam_init=lam_init, online=online)
        return pl.pallas_call(
            kern,
            grid=(b, N_HEADS, lq // tq),
            in_specs=specs,
            out_specs=pl.BlockSpec((None, tq, vd), lambda bb, h, i: (bb, i, h)),
            out_shape=jax.ShapeDtypeStruct((b, lq, d), BF16),
            compiler_params=_params(3),
            name="diff_flash_online" if online else "diff_flash",
        )

    return lax.cond(score_bound <= MAX_FIXED_SHIFT, lambda *a: call(False)(*a), lambda *a: call(True)(*a), *ins)


def _conf_in_kernel(x_ref, mod_ref, g_ref, w_ref, b_ref, u_ref, *, mod_base):
    x = x_ref[...]
    d = x.shape[1]
    shift, scale, _ = _mod_rows(mod_ref, mod_base)
    h = _ada_in(x, g_ref[...], shift, scale).astype(BF16)
    for c in range(0, d, PROJ_CHUNK):
        a = _dot(h, w_ref[:, c:c + PROJ_CHUNK]) + b_ref[:, c:c + PROJ_CHUNK]
        gt = _dot(h, w_ref[:, d + c:d + c + PROJ_CHUNK]) + b_ref[:, d + c:d + c + PROJ_CHUNK]
        u_ref[:, c:c + PROJ_CHUNK] = a * _sigmoid(gt)


def _conf_in(x, mods, g, w_in, b_in, mod_base):
    b, n, d = x.shape
    tm = _token_tile(n)
    kern = functools.partial(_conf_in_kernel, mod_base=mod_base)
    return pl.pallas_call(
        kern,
        grid=(b, n // tm),
        in_specs=[_tok_spec(tm, d), _mod_spec(mods), _const_spec(g), _const_spec(w_in), _const_spec(b_in)],
        out_specs=_tok_spec(tm, d),
        out_shape=jax.ShapeDtypeStruct(x.shape, F32),
        compiler_params=_params(2),
        name="conf_in",
    )(x, mods, g, w_in, b_in)


def _sc_in_kernel(x_ref, mod_ref, g_ref, w_ref, b_out_ref, m_out_ref, *, mod_base):
    x = x_ref[...]
    d = x.shape[1]
    shift, scale, _ = _mod_rows(mod_ref, mod_base)
    h = _ada_in(x, g_ref[...], shift, scale).astype(BF16)
    for c in range(0, d, PROJ_CHUNK):
        b_out_ref[:, c:c + PROJ_CHUNK] = _dot(h, w_ref[:, c:c + PROJ_CHUNK])
        cg = _dot(h, w_ref[:, d + c:d + c + PROJ_CHUNK])
        xh = _dot(h, w_ref[:, 2 * d + c:2 * d + c + PROJ_CHUNK])
        m_out_ref[:, c:c + PROJ_CHUNK] = cg * xh


def _sc_in(x, mods, g, w_in, mod_base):
    b, n, d = x.shape
    tm = _token_tile(n)
    kern = functools.partial(_sc_in_kernel, mod_base=mod_base)
    return pl.pallas_call(
        kern,
        grid=(b, n // tm),
        in_specs=[_tok_spec(tm, d), _mod_spec(mods), _const_spec(g), _const_spec(w_in)],
        out_specs=[_tok_spec(tm, d), _tok_spec(tm, d)],
        out_shape=[jax.ShapeDtypeStruct(x.shape, F32)] * 2,
        compiler_params=_params(2),
        name="sconv_in",
    )(x, mods, g, w_in)


def _dwconv(win_ref, al_ref, cur_ref, prev_ref, next_ref, dw_ref, halo):
    tm = cur_ref.shape[0]
    taps = dw_ref.shape[0]
    i = pl.program_id(1)
    first = i == 0
    last = i == pl.num_programs(1) - 1
    win_ref[0:halo, :] = jnp.where(first, 0.0, prev_ref[...])
    win_ref[halo:halo + tm, :] = cur_ref[...]
    win_ref[halo + tm:, :] = jnp.where(last, 0.0, next_ref[...])
    off = halo - taps // 2
    acc = None
    for s in range(min(SUBLANES, taps)):
        ks = range(s, taps, SUBLANES)
        src, base = win_ref, off + s
        if len(ks) > 1 and base % SUBLANES:
            span = tm + SUBLANES * (len(ks) - 1)
            al_ref[0:span, :] = win_ref[base:base + span, :]
            src, base = al_ref, 0
        for jj, k in enumerate(ks):
            term = src[base + SUBLANES * jj:base + SUBLANES * jj + tm, :] * dw_ref[k:k + 1, :]
            acc = term if acc is None else acc + term
    return acc


def _conf_out_kernel(x_ref, mod_ref, cur_ref, prev_ref, next_ref, dw_ref, dwb_ref, lng_ref, lnb_ref,
                     w_ref, b_ref, o_ref, win_ref, al_ref, *, mod_base, halo):
    u = _dwconv(win_ref, al_ref, cur_ref, prev_ref, next_ref, dw_ref, halo) + dwb_ref[...]
    mu = jnp.mean(u, axis=-1, keepdims=True)
    uc = u - mu
    var = jnp.mean(uc * uc, axis=-1, keepdims=True)
    y = uc * lax.rsqrt(var + EPS) * lng_ref[...] + lnb_ref[...]
    y = (y * _sigmoid(y)).astype(BF16)
    gate = mod_ref[mod_base + 2:mod_base + 3, :]
    o_ref[...] = x_ref[...] + gate * (_dot(y, w_ref[...]) + b_ref[...])


def _sc_out_kernel(x_ref, mod_ref, bg_ref, cur_ref, prev_ref, next_ref, dw_ref, w_ref, o_ref, win_ref,
                   *, mod_base, halo):
    u = _dwconv(win_ref, None, cur_ref, prev_ref, next_ref, dw_ref, halo)
    y = (bg_ref[...] * u).astype(BF16)
    gate = mod_ref[mod_base + 2:mod_base + 3, :]
    o_ref[...] = x_ref[...] + gate * _dot(y, w_ref[...])


def _halo_specs(tm, d, halo, n):
    r = tm // halo
    nb = n // halo
    prev = pl.BlockSpec((None, halo, d), lambda b, i: (b, jnp.maximum(i * r - 1, 0), 0))
    nxt = pl.BlockSpec((None, halo, d), lambda b, i: (b, jnp.minimum((i + 1) * r, nb - 1), 0))
    return prev, nxt


def _conf_out(x, u, mods, dw_w, dw_b, ln_g, ln_b, w_out, b_out, mod_base):
    b, n, d = x.shape
    tm = _token_tile(n)
    halo = 16
    prev, nxt = _halo_specs(tm, d, halo, n)
    kern = functools.partial(_conf_out_kernel, mod_base=mod_base, halo=halo)
    consts = [dw_w, dw_b, ln_g, ln_b, w_out, b_out]
    return pl.pallas_call(
        kern,
        grid=(b, n // tm),
        in_specs=[_tok_spec(tm, d), _mod_spec(mods), _tok_spec(tm, d), prev, nxt] + [_const_spec(a) for a in consts],
        out_specs=_tok_spec(tm, d),
        out_shape=jax.ShapeDtypeStruct(x.shape, F32),
        scratch_shapes=[pltpu.VMEM((tm + 2 * halo, d), F32),
                        pltpu.VMEM((tm + SUBLANES * ((dw_w.shape[0] - 1) // SUBLANES), d), F32)],
        compiler_params=_params(2),
        name="conf_out",
    )(x, mods, u, u, u, *consts)


def _sc_out(x, bg, m, mods, dw_w, w_out, mod_base):
    b, n, d = x.shape
    tm = _token_tile(n)
    halo = 8
    prev, nxt = _halo_specs(tm, d, halo, n)
    kern = functools.partial(_sc_out_kernel, mod_base=mod_base, halo=halo)
    return pl.pallas_call(
        kern,
        grid=(b, n // tm),
        in_specs=[_tok_spec(tm, d), _mod_spec(mods), _tok_spec(tm, d), _tok_spec(tm, d), prev, nxt,
                  _const_spec(dw_w), _const_spec(w_out)],
        out_specs=_tok_spec(tm, d),
        out_shape=jax.ShapeDtypeStruct(x.shape, F32),
        scratch_shapes=[pltpu.VMEM((tm + 2 * halo, d), F32)],
        compiler_params=_params(2),
        name="sconv_out",
    )(x, mods, bg, m, m, m, dw_w, w_out)


def _rope_tables(n_tokens, head_dim):
    rows = n_tokens // GRID_W
    row_ids = jnp.repeat(jnp.arange(rows, dtype=F32), GRID_W)
    col_ids = jnp.tile(jnp.arange(GRID_W, dtype=F32), rows)
    half = head_dim // 2
    inv_freq = ROPE_THETA ** (-jnp.arange(0, half, 2, dtype=F32) / half)
    ang_r = row_ids[:, None] * inv_freq
    ang_c = col_ids[:, None] * inv_freq
    ang = jnp.concatenate([ang_r, ang_r, ang_c, ang_c], axis=-1)
    quarter = head_dim // 4
    sign = jnp.where((jnp.arange(head_dim) % (2 * quarter)) < quarter, -1.0, 1.0).astype(F32)
    reps = LANES // head_dim
    return jnp.tile(jnp.cos(ang), (1, reps)), jnp.tile(jnp.sin(ang) * sign, (1, reps))


def _group_mean_matrix(head_dim):
    idx = jnp.arange(MXU_DIM) // head_dim
    return jnp.where(idx[:, None] == idx[None, :], 1.0 / head_dim, 0.0).astype(BF16)


def kernel(x, c, ctx, c_ctx, ada_w, ada_b, norm_g, ffn_w_in, ffn_w_out, attn_w_qkv, attn_w_o, attn_q_g,
           attn_k_g, attn_lambda, attn_subln_g, conv_w_in, conv_b_in, conv_dw_w, conv_dw_b, conv_ln_g,
           conv_ln_b, conv_w_out, conv_b_out, sc_w_in, sc_dw_w, sc_w_out):
    bsz, seq, d = x.shape
    depth = ada_w.shape[0]
    head_dim = d // (2 * N_HEADS)
    row = lambda v: v.reshape(1, -1)

    n_rows = ((bsz + 1 + 7) // 8) * 8
    cvec = jnp.zeros((n_rows, d), F32).at[:bsz].set(c).at[bsz].set(c_ctx)
    mods = _modulations(cvec, ada_w, ada_b).reshape(depth, n_rows, N_MOD, d)

    rope = _rope_tables(seq, head_dim)
    gmat = _group_mean_matrix(head_dim)

    for i in range(depth):
        kind = i % N_MIXERS
        j = i // N_MIXERS
        last = i == depth - 1
        ctx_in = (not last) or kind == 0
        ctx_out = not last
        ml = mods[i, :bsz]
        mc = mods[i, bsz:bsz + 1]
        g = [row(norm_g[i, s]) for s in range(3)]
        w_in = [ffn_w_in[i, s].astype(BF16) for s in range(2)]
        w_out = [ffn_w_out[i, s].astype(BF16) for s in range(2)]

        x = _ffn(x, ml, g[0], w_in[0], w_out[0], 0)
        if ctx_in:
            ctx = _ffn(ctx, mc, g[0], w_in[0], w_out[0], 0)

        if kind == 0:
            lam_init = 0.8 - 0.6 * math.exp(-0.3 * i)
            w_qkv = attn_w_qkv[j].astype(BF16)
            w_o = attn_w_o[j].astype(BF16)
            qg = jnp.tile(attn_q_g[j], MXU_DIM // head_dim).reshape(1, MXU_DIM)
            kg = jnp.tile(attn_k_g[j], MXU_DIM // head_dim).reshape(1, MXU_DIM)
            subg = attn_subln_g[j].reshape(-1, 1)
            bound = (1.02 * LOG2E * head_dim ** 0.5) * jnp.max(jnp.abs(attn_q_g[j])) * jnp.max(jnp.abs(attn_k_g[j]))
            q_l, k_l, vt_l = _qkv_proj(x, ml, g[1], w_qkv, gmat, qg, kg, rope, 3)
            q_c, k_c, vt_c = _qkv_proj(ctx, mc, g[1], w_qkv, gmat, qg, kg, None, 3)
            o_l = _flash(q_l, [(k_l, vt_l), (k_c, vt_c)], attn_lambda[j], subg, bound, lam_init)
            x = _attn_out_ffn(x, o_l, ml, w_o, g[2], w_in[1], w_out[1], 3)
            if ctx_out:
                o_c = _flash(q_c, [(k_c, vt_c)], attn_lambda[j], subg, bound, lam_init)
                ctx = _attn_out_ffn(ctx, o_c, mc, w_o, g[2], w_in[1], w_out[1], 3)
        elif kind == 1:
            cw_in = conv_w_in[j].astype(BF16)
            cw_out = conv_w_out[j].astype(BF16)
            cargs = (conv_dw_w[j], row(conv_dw_b[j]), row(conv_ln_g[j]), row(conv_ln_b[j]), cw_out,
                     row(conv_b_out[j]))
            u = _conf_in(x, ml, g[1], cw_in, row(conv_b_in[j]), 3)
            x = _conf_out(x, u, ml, *cargs, 3)
            if ctx_out:
                u = _conf_in(ctx, mc, g[1], cw_in, row(conv_b_in[j]), 3)
                ctx = _conf_out(ctx, u, mc, *cargs, 3)
        else:
            sw_in = sc_w_in[j].astype(BF16)
            sw_out = sc_w_out[j].astype(BF16)
            bg, m = _sc_in(x, ml, g[1], sw_in, 3)
            x = _sc_out(x, bg, m, ml, sc_dw_w[j], sw_out, 3)
            if ctx_out:
                bg, m = _sc_in(ctx, mc, g[1], sw_in, 3)
                ctx = _sc_out(ctx, bg, m, mc, sc_dw_w[j], sw_out, 3)

        if kind != 0:
            x = _ffn(x, ml, g[2], w_in[1], w_out[1], 6)
            if ctx_out:
                ctx = _ffn(ctx, mc, g[2], w_in[1], w_out[1], 6)

    return x
```

```python
import functools
import math

import jax
import jax.numpy as jnp
from jax import lax
from jax.experimental import pallas as pl
from jax.experimental.pallas import tpu as pltpu

GRID_W = 64
N_MIXERS = 3
N_HEADS = 8
N_MOD = 9
ROPE_THETA = 10000.0
EPS = 1e-6

LANES = 128
SUBLANES = 8
MXU_DIM = 256
VMEM_LIMIT = 56 * 1024 * 1024
TOKEN_TILE = 512
FFN_TILE = 1024
FLASH_TQ = 1024
FLASH_TK = 2048
PROJ_CHUNK = 2 * MXU_DIM
MAX_FIXED_SHIFT = 60.0
LOG2E = math.log2(math.e)

BF16 = jnp.bfloat16
F32 = jnp.float32


def _dot(a, b):
    return jnp.dot(a, b, preferred_element_type=F32)


def _dot_nt(a, b):
    return lax.dot_general(a, b, (((1,), (1,)), ((), ())), preferred_element_type=F32)


def _sigmoid(x):
    return 1.0 / (1.0 + jnp.exp(-x))


def _ada_in(x, g, shift, scale):
    y = x * lax.rsqrt(jnp.mean(x * x, axis=-1, keepdims=True) + EPS)
    return (y * g) * (1.0 + scale) + shift


def _mod_rows(mod_ref, base):
    return (mod_ref[base:base + 1, :], mod_ref[base + 1:base + 2, :], mod_ref[base + 2:base + 3, :])


def _tok_spec(tm, d):
    return pl.BlockSpec((None, tm, d), lambda b, i: (b, i, 0))


def _mod_spec(mods):
    per_batch = mods.shape[0] > 1
    return pl.BlockSpec((None,) + mods.shape[1:], lambda b, i: (b if per_batch else 0, 0, 0))


def _const_spec(arr):
    nd = arr.ndim
    return pl.BlockSpec(arr.shape, lambda *_: (0,) * nd, pipeline_mode=pl.Buffered(1))


def _params(n_grid):
    return pltpu.CompilerParams(dimension_semantics=("parallel",) * n_grid,
                                vmem_limit_bytes=VMEM_LIMIT)


def _token_tile(n):
    return TOKEN_TILE if n % TOKEN_TILE == 0 else n


def _ffn_tile(n):
    return FFN_TILE if n % FFN_TILE == 0 else n


def _mods_kernel(c_ref, w_ref, b_ref, o_ref):
    c = c_ref[...]
    sc = (c * _sigmoid(c)).astype(BF16)
    o_ref[...] = _dot(sc, w_ref[...].astype(BF16)) + b_ref[...]


def _modulations(cvec, ada_w, ada_b):
    depth, d, n = ada_w.shape
    r = cvec.shape[0]
    tn = n // 8
    return pl.pallas_call(
        _mods_kernel,
        grid=(depth, n // tn),
        in_specs=[pl.BlockSpec((r, d), lambda l, j: (0, 0)),
                  pl.BlockSpec((None, d, tn), lambda l, j: (l, 0, j)),
                  pl.BlockSpec((None, 1, tn), lambda l, j: (l, 0, j))],
        out_specs=pl.BlockSpec((None, r, tn), lambda l, j: (l, 0, j)),
        out_shape=jax.ShapeDtypeStruct((depth, r, n), F32),
        compiler_params=_params(2),
        name="adaln_mods",
    )(cvec, ada_w, ada_b.reshape(depth, 1, n))


def _swiglu_half_step(x, mod_ref, mod_base, g_ref, win_ref, wout_ref, u_ref):
    shift, scale, gate = _mod_rows(mod_ref, mod_base)
    h = _ada_in(x, g_ref[...], shift, scale).astype(BF16)
    f = wout_ref.shape[0]
    for c in range(0, f, MXU_DIM):
        a = _dot(h, win_ref[:, c:c + MXU_DIM])
        gt = _dot(h, win_ref[:, f + c:f + c + MXU_DIM])
        u_ref[:, c:c + MXU_DIM] = ((gt * _sigmoid(gt)) * a).astype(BF16)
    return x + (0.5 * gate) * _dot(u_ref[...], wout_ref[...])


def _ffn_kernel(x_ref, mod_ref, g_ref, win_ref, wout_ref, o_ref, u_ref, *, mod_base):
    o_ref[...] = _swiglu_half_step(x_ref[...], mod_ref, mod_base, g_ref, win_ref, wout_ref, u_ref)


def _attn_out_ffn_kernel(x_ref, a_ref, mod_ref, wo_ref, g_ref, win_ref, wout_ref, o_ref, u_ref, *, mod_base):
    gate = mod_ref[mod_base + 2:mod_base + 3, :]
    x = x_ref[...] + gate * _dot(a_ref[...], wo_ref[...])
    o_ref[...] = _swiglu_half_step(x, mod_ref, mod_base + 3, g_ref, win_ref, wout_ref, u_ref)


def _ffn(x, mods, g, w_in, w_out, mod_base):
    b, n, d = x.shape
    f = w_out.shape[0]
    assert f % MXU_DIM == 0
    tm = _ffn_tile(n)
    kern = functools.partial(_ffn_kernel, mod_base=mod_base)
    return pl.pallas_call(
        kern,
        grid=(b, n // tm),
        in_specs=[_tok_spec(tm, d), _mod_spec(mods), _const_spec(g), _const_spec(w_in), _const_spec(w_out)],
        out_specs=_tok_spec(tm, d),
        out_shape=jax.ShapeDtypeStruct(x.shape, F32),
        scratch_shapes=[pltpu.VMEM((tm, f), BF16)],
        compiler_params=_params(2),
        name="ffn",
    )(x, mods, g, w_in, w_out)


def _attn_out_ffn(x, attn, mods, w_o, g, w_in, w_out, mod_base):
    b, n, d = x.shape
    f = w_out.shape[0]
    assert f % MXU_DIM == 0
    tm = _ffn_tile(n)
    kern = functools.partial(_attn_out_ffn_kernel, mod_base=mod_base)
    return pl.pallas_call(
        kern,
        grid=(b, n // tm),
        in_specs=[_tok_spec(tm, d), _tok_spec(tm, d), _mod_spec(mods), _const_spec(w_o), _const_spec(g),
                  _const_spec(w_in), _const_spec(w_out)],
        out_specs=_tok_spec(tm, d),
        out_shape=jax.ShapeDtypeStruct(x.shape, F32),
        scratch_shapes=[pltpu.VMEM((tm, f), BF16)],
        compiler_params=_params(2),
        name="attn_out_ffn",
    )(x, attn, mods, w_o, g, w_in, w_out)


def _qkv_kernel(*refs, mod_base, use_rope):
    if use_rope:
        (x_ref, mod_ref, g_ref, w_ref, gm_ref, qg_ref, kg_ref, cos_ref, sin_ref,
         q_ref, k_ref, vt_ref) = refs
    else:
        x_ref, mod_ref, g_ref, w_ref, gm_ref, qg_ref, kg_ref, q_ref, k_ref, vt_ref = refs
    x = x_ref[...]
    d = x.shape[1]
    shift, scale, _ = _mod_rows(mod_ref, mod_base)
    h = _ada_in(x, g_ref[...], shift, scale).astype(BF16)
    if use_rope:
        cos = cos_ref[...]
        sin = sin_ref[...]
        lane = lax.broadcasted_iota(jnp.int32, cos.shape, 1)
        low_half = (lane % 32) < 16

    def norm_rope(col0, gain_ref, out_ref, post_scale):
        for cc in range(0, d, PROJ_CHUNK):
            t = _dot(h, w_ref[:, col0 + cc:col0 + cc + PROJ_CHUNK])
            for p in range(0, PROJ_CHUNK, MXU_DIM):
                sl = t[:, p:p + MXU_DIM]
                ms = _dot((sl * sl).astype(BF16), gm_ref[...])
                sl = sl * lax.rsqrt(ms + EPS) * gain_ref[...]
                for hh in range(0, MXU_DIM, LANES):
                    y = sl[:, hh:hh + LANES]
                    if use_rope:
                        rot = jnp.where(low_half, pltpu.roll(y, LANES - 16, 1), pltpu.roll(y, 16, 1))
                        y = y * cos + rot * sin
                    c0 = cc + p + hh
                    out_ref[:, c0:c0 + LANES] = (y * post_scale).astype(BF16)

    head_dim = d // (2 * N_HEADS)
    norm_rope(0, qg_ref, q_ref, head_dim ** -0.5 * LOG2E)
    norm_rope(d, kg_ref, k_ref, 1.0)
    for cc in range(0, d, PROJ_CHUNK):
        v = _dot(h, w_ref[:, 2 * d + cc:2 * d + cc + PROJ_CHUNK])
        vt_ref[cc:cc + PROJ_CHUNK, :] = v.T.astype(BF16)


def _qkv_proj(x, mods, g, w_qkv, gmat, qg, kg, rope, mod_base):
    b, n, d = x.shape
    tm = _token_tile(n)
    use_rope = rope is not None
    ins = [x, mods, g, w_qkv, gmat, qg, kg]
    specs = [_tok_spec(tm, d), _mod_spec(mods), _const_spec(g), _const_spec(w_qkv), _const_spec(gmat),
             _const_spec(qg), _const_spec(kg)]
    if use_rope:
        ins += list(rope)
        specs += [pl.BlockSpec((tm, LANES), lambda bb, i: (i, 0))] * 2
    kern = functools.partial(_qkv_kernel, mod_base=mod_base, use_rope=use_rope)
    return pl.pallas_call(
        kern,
        grid=(b, n // tm),
        in_specs=specs,
        out_specs=[_tok_spec(tm, d), _tok_spec(tm, d),
                   pl.BlockSpec((None, d, tm), lambda bb, i: (bb, 0, i))],
        out_shape=[jax.ShapeDtypeStruct((b, n, d), BF16), jax.ShapeDtypeStruct((b, n, d), BF16),
                   jax.ShapeDtypeStruct((b, d, n), BF16)],
        compiler_params=_params(2),
        name="qkv_proj",
    )(*ins)


def _flash_kernel(*refs, n_seg, tk, lam_init, online):
    shift_ref, lamp_ref, subg_ref, q_ref = refs[:4]
    seg_refs = refs[4:4 + 2 * n_seg]
    o_ref = refs[4 + 2 * n_seg]

    lp = lamp_ref[...]
    lam = (jnp.exp(jnp.sum(lp[0:1] * lp[1:2], keepdims=True))
           - jnp.exp(jnp.sum(lp[2:3] * lp[3:4], keepdims=True)) + lam_init)

    q = q_ref[...]
    tq, vd = q.shape
    lane = lax.broadcasted_iota(jnp.int32, q.shape, 1)
    zero = jnp.zeros_like(q)
    qm = (jnp.where(lane < vd // 2, q, zero), jnp.where(lane >= vd // 2, q, zero))

    if online:
        m = [jnp.full((1, tq), -jnp.inf, F32) for _ in range(2)]
        l = [jnp.zeros((1, tq), F32) for _ in range(2)]
        acc = [jnp.zeros((vd, tq), F32) for _ in range(2)]
    else:
        shift = shift_ref[0]
        q2 = jnp.concatenate(qm, axis=0)
        l2 = jnp.zeros((8, 2 * tq), F32)
        acc2 = jnp.zeros((vd, 2 * tq), F32)
    for s in range(n_seg):
        k_ref, vt_ref = seg_refs[2 * s], seg_refs[2 * s + 1]
        nk = k_ref.shape[0]
        ck = min(tk, nk)
        for c in range(nk // ck):
            kch = k_ref[c * ck:(c + 1) * ck, :]
            vtch = vt_ref[:, c * ck:(c + 1) * ck]
            if online:
                for j in range(2):
                    st = _dot_nt(kch, qm[j])
                    m_new = jnp.maximum(m[j], jnp.max(st, axis=0, keepdims=True))
                    alpha = jnp.exp2(m[j] - m_new)
                    p = jnp.exp2(st - m_new)
                    l[j] = alpha * l[j] + jnp.sum(p, axis=0, keepdims=True)
                    acc[j] = alpha * acc[j] + _dot(vtch, p.astype(BF16))
                    m[j] = m_new
            else:
                p = jnp.exp2(_dot_nt(kch, q2) - shift)
                l2 = l2 + jnp.sum(p.reshape(ck // 8, 8, 2 * tq), axis=0)
                acc2 = acc2 + _dot(vtch, p.astype(BF16))

    if not online:
        l2 = jnp.sum(l2, axis=0, keepdims=True)
        l = [l2[:, :tq], l2[:, tq:]]
        acc = [acc2[:, :tq], acc2[:, tq:]]
    ot = acc[0] / l[0] - lam * (acc[1] / l[1])
    ot = ot * lax.rsqrt(jnp.mean(ot * ot, axis=0, keepdims=True) + EPS)
    ot = ot * subg_ref[...] * (1.0 - lam_init)
    o_ref[...] = ot.T.astype(BF16)


def _flash(q, segs, lam_p, subg, score_bound, lam_init):
    b, lq, d = q.shape
    vd = d // N_HEADS
    tq = min(FLASH_TQ, lq)
    ins = [score_bound.reshape(1), lam_p, subg, q]
    specs = [pl.BlockSpec(memory_space=pltpu.SMEM),
             pl.BlockSpec(lam_p.shape, lambda bb, h, i: (0, 0)),
             pl.BlockSpec(subg.shape, lambda bb, h, i: (0, 0)),
             pl.BlockSpec((None, tq, vd), lambda bb, h, i: (bb, i, h))]
    for k, vt in segs:
        nk = k.shape[1]
        ins += [k, vt]
        specs += [pl.BlockSpec((None, nk, vd), lambda bb, h, i: (bb, 0, h)),
                  pl.BlockSpec((None, vd, nk), lambda bb, h, i: (bb, h, 0))]

    def call(online):
        kern = functools.partial(_flash_kernel, n_seg=len(segs), tk=FLASH_TK, lam_init=lam_init, online=online)
        return pl.pallas_call(
            kern,
            grid=(b, N_HEADS, lq // tq),
            in_specs=specs,
            out_specs=pl.BlockSpec((None, tq, vd), lambda bb, h, i: (bb, i, h)),
            out_shape=jax.ShapeDtypeStruct((b, lq, d), BF16),
            compiler_params=_params(3),
            name="diff_flash_online" if online else "diff_flash",
        )

    return lax.cond(score_bound <= MAX_FIXED_SHIFT, lambda *a: call(False)(*a), lambda *a: call(True)(*a), *ins)


def _conf_in_kernel(x_ref, mod_ref, g_ref, w_ref, b_ref, u_ref, *, mod_base):
    x = x_ref[...]
    d = x.shape[1]
    shift, scale, _ = _mod_rows(mod_ref, mod_base)
    h = _ada_in(x, g_ref[...], shift, scale).astype(BF16)
    for c in range(0, d, PROJ_CHUNK):
        a = _dot(h, w_ref[:, c:c + PROJ_CHUNK]) + b_ref[:, c:c + PROJ_CHUNK]
        gt = _dot(h, w_ref[:, d + c:d + c + PROJ_CHUNK]) + b_ref[:, d + c:d + c + PROJ_CHUNK]
        u_ref[:, c:c + PROJ_CHUNK] = a * _sigmoid(gt)


def _conf_in(x, mods, g, w_in, b_in, mod_base):
    b, n, d = x.shape
    tm = _token_tile(n)
    kern = functools.partial(_conf_in_kernel, mod_base=mod_base)
    return pl.pallas_call(
        kern,
        grid=(b, n // tm),
        in_specs=[_tok_spec(tm, d), _mod_spec(mods), _const_spec(g), _const_spec(w_in), _const_spec(b_in)],
        out_specs=_tok_spec(tm, d),
        out_shape=jax.ShapeDtypeStruct(x.shape, F32),
        compiler_params=_params(2),
        name="conf_in",
    )(x, mods, g, w_in, b_in)


def _sc_in_kernel(x_ref, mod_ref, g_ref, w_ref, b_out_ref, m_out_ref, *, mod_base):
    x = x_ref[...]
    d = x.shape[1]
    shift, scale, _ = _mod_rows(mod_ref, mod_base)
    h = _ada_in(x, g_ref[...], shift, scale).astype(BF16)
    for c in range(0, d, PROJ_CHUNK):
        b_out_ref[:, c:c + PROJ_CHUNK] = _dot(h, w_ref[:, c:c + PROJ_CHUNK])
        cg = _dot(h, w_ref[:, d + c:d + c + PROJ_CHUNK])
        xh = _dot(h, w_ref[:, 2 * d + c:2 * d + c + PROJ_CHUNK])
        m_out_ref[:, c:c + PROJ_CHUNK] = cg * xh


def _sc_in(x, mods, g, w_in, mod_base):
    b, n, d = x.shape
    tm = _token_tile(n)
    kern = functools.partial(_sc_in_kernel, mod_base=mod_base)
    return pl.pallas_call(
        kern,
        grid=(b, n // tm),
        in_specs=[_tok_spec(tm, d), _mod_spec(mods), _const_spec(g), _const_spec(w_in)],
        out_specs=[_tok_spec(tm, d), _tok_spec(tm, d)],
        out_shape=[jax.ShapeDtypeStruct(x.shape, F32)] * 2,
        compiler_params=_params(2),
        name="sconv_in",
    )(x, mods, g, w_in)


def _dwconv(win_ref, al_ref, cur_ref, prev_ref, next_ref, dw_ref, halo):
    tm = cur_ref.shape[0]
    taps = dw_ref.shape[0]
    i = pl.program_id(1)
    first = i == 0
    last = i == pl.num_programs(1) - 1
    win_ref[0:halo, :] = jnp.where(first, 0.0, prev_ref[...])
    win_ref[halo:halo + tm, :] = cur_ref[...]
    win_ref[halo + tm:, :] = jnp.where(last, 0.0, next_ref[...])
    off = halo - taps // 2
    acc = None
    for s in range(min(SUBLANES, taps)):
        ks = range(s, taps, SUBLANES)
        src, base = win_ref, off + s
        if len(ks) > 1 and base % SUBLANES:
            span = tm + SUBLANES * (len(ks) - 1)
            al_ref[0:span, :] = win_ref[base:base + span, :]
            src, base = al_ref, 0
        for jj, k in enumerate(ks):
            term = src[base + SUBLANES * jj:base + SUBLANES * jj + tm, :] * dw_ref[k:k + 1, :]
            acc = term if acc is None else acc + term
    return acc


def _conf_out_kernel(x_ref, mod_ref, cur_ref, prev_ref, next_ref, dw_ref, dwb_ref, lng_ref, lnb_ref,
                     w_ref, b_ref, o_ref, win_ref, al_ref, *, mod_base, halo):
    u = _dwconv(win_ref, al_ref, cur_ref, prev_ref, next_ref, dw_ref, halo) + dwb_ref[...]
    mu = jnp.mean(u, axis=-1, keepdims=True)
    uc = u - mu
    var = jnp.mean(uc * uc, axis=-1, keepdims=True)
    y = uc * lax.rsqrt(var + EPS) * lng_ref[...] + lnb_ref[...]
    y = (y * _sigmoid(y)).astype(BF16)
    gate = mod_ref[mod_base + 2:mod_base + 3, :]
    o_ref[...] = x_ref[...] + gate * (_dot(y, w_ref[...]) + b_ref[...])


def _sc_out_kernel(x_ref, mod_ref, bg_ref, cur_ref, prev_ref, next_ref, dw_ref, w_ref, o_ref, win_ref,
                   *, mod_base, halo):
    u = _dwconv(win_ref, None, cur_ref, prev_ref, next_ref, dw_ref, halo)
    y = (bg_ref[...] * u).astype(BF16)
    gate = mod_ref[mod_base + 2:mod_base + 3, :]
    o_ref[...] = x_ref[...] + gate * _dot(y, w_ref[...])


def _halo_specs(tm, d, halo, n):
    r = tm // halo
    nb = n // halo
    prev = pl.BlockSpec((None, halo, d), lambda b, i: (b, jnp.maximum(i * r - 1, 0), 0))
    nxt = pl.BlockSpec((None, halo, d), lambda b, i: (b, jnp.minimum((i + 1) * r, nb - 1), 0))
    return prev, nxt


def _conf_out(x, u, mods, dw_w, dw_b, ln_g, ln_b, w_out, b_out, mod_base):
    b, n, d = x.shape
    tm = _token_tile(n)
    halo = 16
    prev, nxt = _halo_specs(tm, d, halo, n)
    kern = functools.partial(_conf_out_kernel, mod_base=mod_base, halo=halo)
    consts = [dw_w, dw_b, ln_g, ln_b, w_out, b_out]
    return pl.pallas_call(
        kern,
        grid=(b, n // tm),
        in_specs=[_tok_spec(tm, d), _mod_spec(mods), _tok_spec(tm, d), prev, nxt] + [_const_spec(a) for a in consts],
        out_specs=_tok_spec(tm, d),
        out_shape=jax.ShapeDtypeStruct(x.shape, F32),
        scratch_shapes=[pltpu.VMEM((tm + 2 * halo, d), F32),
                        pltpu.VMEM((tm + SUBLANES * ((dw_w.shape[0] - 1) // SUBLANES), d), F32)],
        compiler_params=_params(2),
        name="conf_out",
    )(x, mods, u, u, u, *consts)


def _sc_out(x, bg, m, mods, dw_w, w_out, mod_base):
    b, n, d = x.shape
    tm = _token_tile(n)
    halo = 8
    prev, nxt = _halo_specs(tm, d, halo, n)
    kern = functools.partial(_sc_out_kernel, mod_base=mod_base, halo=halo)
    return pl.pallas_call(
        kern,
        grid=(b, n // tm),
        in_specs=[_tok_spec(tm, d), _mod_spec(mods), _tok_spec(tm, d), _tok_spec(tm, d), prev, nxt,
                  _const_spec(dw_w), _const_spec(w_out)],
        out_specs=_tok_spec(tm, d),
        out_shape=jax.ShapeDtypeStruct(x.shape, F32),
        scratch_shapes=[pltpu.VMEM((tm + 2 * halo, d), F32)],
        compiler_params=_params(2),
        name="sconv_out",
    )(x, mods, bg, m, m, m, dw_w, w_out)


def _rope_tables(n_tokens, head_dim):
    rows = n_tokens // GRID_W
    row_ids = jnp.repeat(jnp.arange(rows, dtype=F32), GRID_W)
    col_ids = jnp.tile(jnp.arange(GRID_W, dtype=F32), rows)
    half = head_dim // 2
    inv_freq = ROPE_THETA ** (-jnp.arange(0, half, 2, dtype=F32) / half)
    ang_r = row_ids[:, None] * inv_freq
    ang_c = col_ids[:, None] * inv_freq
    ang = jnp.concatenate([ang_r, ang_r, ang_c, ang_c], axis=-1)
    quarter = head_dim // 4
    sign = jnp.where((jnp.arange(head_dim) % (2 * quarter)) < quarter, -1.0, 1.0).astype(F32)
    reps = LANES // head_dim
    return jnp.tile(jnp.cos(ang), (1, reps)), jnp.tile(jnp.sin(ang) * sign, (1, reps))


def _group_mean_matrix(head_dim):
    idx = jnp.arange(MXU_DIM) // head_dim
    return jnp.where(idx[:, None] == idx[None, :], 1.0 / head_dim, 0.0).astype(BF16)


def kernel(x, c, ctx, c_ctx, ada_w, ada_b, norm_g, ffn_w_in, ffn_w_out, attn_w_qkv, attn_w_o, attn_q_g,
           attn_k_g, attn_lambda, attn_subln_g, conv_w_in, conv_b_in, conv_dw_w, conv_dw_b, conv_ln_g,
           conv_ln_b, conv_w_out, conv_b_out, sc_w_in, sc_dw_w, sc_w_out):
    bsz, seq, d = x.shape
    depth = ada_w.shape[0]
    head_dim = d // (2 * N_HEADS)
    row = lambda v: v.reshape(1, -1)

    n_rows = ((bsz + 1 + 7) // 8) * 8
    cvec = jnp.zeros((n_rows, d), F32).at[:bsz].set(c).at[bsz].set(c_ctx)
    mods = _modulations(cvec, ada_w, ada_b).reshape(depth, n_rows, N_MOD, d)

    rope = _rope_tables(seq, head_dim)
    gmat = _group_mean_matrix(head_dim)

    for i in range(depth):
        kind = i % N_MIXERS
        j = i // N_MIXERS
        last = i == depth - 1
        ctx_in = (not last) or kind == 0
        ctx_out = not last
        ml = mods[i, :bsz]
        mc = mods[i, bsz:bsz + 1]
        g = [row(norm_g[i, s]) for s in range(3)]
        w_in = [ffn_w_in[i, s].astype(BF16) for s in range(2)]
        w_out = [ffn_w_out[i, s].astype(BF16) for s in range(2)]

        x = _ffn(x, ml, g[0], w_in[0], w_out[0], 0)
        if ctx_in:
            ctx = _ffn(ctx, mc, g[0], w_in[0], w_out[0], 0)

        if kind == 0:
            lam_init = 0.8 - 0.6 * math.exp(-0.3 * i)
            w_qkv = attn_w_qkv[j].astype(BF16)
            w_o = attn_w_o[j].astype(BF16)
            qg = jnp.tile(attn_q_g[j], MXU_DIM // head_dim).reshape(1, MXU_DIM)
            kg = jnp.tile(attn_k_g[j], MXU_DIM // head_dim).reshape(1, MXU_DIM)
            subg = attn_subln_g[j].reshape(-1, 1)
            bound = (1.02 * LOG2E * head_dim ** 0.5) * jnp.max(jnp.abs(attn_q_g[j])) * jnp.max(jnp.abs(attn_k_g[j]))
            q_l, k_l, vt_l = _qkv_proj(x, ml, g[1], w_qkv, gmat, qg, kg, rope, 3)
            q_c, k_c, vt_c = _qkv_proj(ctx, mc, g[1], w_qkv, gmat, qg, kg, None, 3)
            o_l = _flash(q_l, [(k_l, vt_l), (k_c, vt_c)], attn_lambda[j], subg, bound, lam_init)
            x = _attn_out_ffn(x, o_l, ml, w_o, g[2], w_in[1], w_out[1], 3)
            if ctx_out:
                o_c = _flash(q_c, [(k_c, vt_c)], attn_lambda[j], subg, bound, lam_init)
                ctx = _attn_out_ffn(ctx, o_c, mc, w_o, g[2], w_in[1], w_out[1], 3)
        elif kind == 1:
            cw_in = conv_w_in[j].astype(BF16)
            cw_out = conv_w_out[j].astype(BF16)
            cargs = (conv_dw_w[j], row(conv_dw_b[j]), row(conv_ln_g[j]), row(conv_ln_b[j]), cw_out,
                     row(conv_b_out[j]))
            u = _conf_in(x, ml, g[1], cw_in, row(conv_b_in[j]), 3)
            x = _conf_out(x, u, ml, *cargs, 3)
            if ctx_out:
                u = _conf_in(ctx, mc, g[1], cw_in, row(conv_b_in[j]), 3)
                ctx = _conf_out(ctx, u, mc, *cargs, 3)
        else:
            sw_in = sc_w_in[j].astype(BF16)
            sw_out = sc_w_out[j].astype(BF16)
            bg, m = _sc_in(x, ml, g[1], sw_in, 3)
            x = _sc_out(x, bg, m, ml, sc_dw_w[j], sw_out, 3)
            if ctx_out:
                bg, m = _sc_in(ctx, mc, g[1], sw_in, 3)
                ctx = _sc_out(ctx, bg, m, mc, sc_dw_w[j], sw_out, 3)

        if kind != 0:
            x = _ffn(x, ml, g[2], w_in[1], w_out[1], 6)
            if ctx_out:
                ctx = _ffn(ctx, mc, g[2], w_in[1], w_out[1], 6)

    return x
```

```python
import functools
import math

import jax
import jax.numpy as jnp
from jax import lax
from jax.experimental import pallas as pl
from jax.experimental.pallas import tpu as pltpu

GRID_W = 64
N_MIXERS = 3
N_HEADS = 8
N_MOD = 9
ROPE_THETA = 10000.0
EPS = 1e-6

LANES = 128
SUBLANES = 8
MXU_DIM = 256
VMEM_LIMIT = 56 * 1024 * 1024
TOKEN_TILE = 512
FFN_TILE = 1024
FLASH_TQ = 1024
FLASH_TK = 2048
PROJ_CHUNK = 2 * MXU_DIM
MAX_FIXED_SHIFT = 60.0
LOG2E = math.log2(math.e)

BF16 = jnp.bfloat16
F32 = jnp.float32


def _dot(a, b):
    return jnp.dot(a, b, preferred_element_type=F32)


def _dot_nt(a, b):
    return lax.dot_general(a, b, (((1,), (1,)), ((), ())), preferred_element_type=F32)


def _sigmoid(x):
    return 1.0 / (1.0 + jnp.exp(-x))


def _ada_in(x, g, shift, scale):
    y = x * lax.rsqrt(jnp.mean(x * x, axis=-1, keepdims=True) + EPS)
    return (y * g) * (1.0 + scale) + shift


def _mod_rows(mod_ref, base):
    return (mod_ref[base:base + 1, :], mod_ref[base + 1:base + 2, :], mod_ref[base + 2:base + 3, :])


def _tok_spec(tm, d):
    return pl.BlockSpec((None, tm, d), lambda b, i: (b, i, 0))


def _mod_spec(mods):
    per_batch = mods.shape[0] > 1
    return pl.BlockSpec((None,) + mods.shape[1:], lambda b, i: (b if per_batch else 0, 0, 0))


def _const_spec(arr):
    nd = arr.ndim
    return pl.BlockSpec(arr.shape, lambda *_: (0,) * nd, pipeline_mode=pl.Buffered(1))


def _params(n_grid):
    return pltpu.CompilerParams(dimension_semantics=("parallel",) * n_grid,
                                vmem_limit_bytes=VMEM_LIMIT)


def _token_tile(n):
    return TOKEN_TILE if n % TOKEN_TILE == 0 else n


def _ffn_tile(n):
    return FFN_TILE if n % FFN_TILE == 0 else n


def _mods_kernel(c_ref, w_ref, b_ref, o_ref):
    c = c_ref[...]
    sc = (c * _sigmoid(c)).astype(BF16)
    o_ref[...] = _dot(sc, w_ref[...].astype(BF16)) + b_ref[...]


def _modulations(cvec, ada_w, ada_b):
    depth, d, n = ada_w.shape
    r = cvec.shape[0]
    tn = n // 8
    return pl.pallas_call(
        _mods_kernel,
        grid=(depth, n // tn),
        in_specs=[pl.BlockSpec((r, d), lambda l, j: (0, 0)),
                  pl.BlockSpec((None, d, tn), lambda l, j: (l, 0, j)),
                  pl.BlockSpec((None, 1, tn), lambda l, j: (l, 0, j))],
        out_specs=pl.BlockSpec((None, r, tn), lambda l, j: (l, 0, j)),
        out_shape=jax.ShapeDtypeStruct((depth, r, n), F32),
        compiler_params=_params(2),
        name="adaln_mods",
    )(cvec, ada_w, ada_b.reshape(depth, 1, n))


def _swiglu_half_step(x, mod_ref, mod_base, g_ref, win_ref, wout_ref, u_ref):
    shift, scale, gate = _mod_rows(mod_ref, mod_base)
    h = _ada_in(x, g_ref[...], shift, scale).astype(BF16)
    f = wout_ref.shape[0]
    for c in range(0, f, MXU_DIM):
        a = _dot(h, win_ref[:, c:c + MXU_DIM])
        gt = _dot(h, win_ref[:, f + c:f + c + MXU_DIM])
        u_ref[:, c:c + MXU_DIM] = ((gt * _sigmoid(gt)) * a).astype(BF16)
    return x + (0.5 * gate) * _dot(u_ref[...], wout_ref[...])


def _ffn_kernel(x_ref, mod_ref, g_ref, win_ref, wout_ref, o_ref, u_ref, *, mod_base):
    o_ref[...] = _swiglu_half_step(x_ref[...], mod_ref, mod_base, g_ref, win_ref, wout_ref, u_ref)


def _tail_ffn_kernel(*refs, kind, mod_base, halo):
    if kind == "attn":
        x_ref, a_ref, mod_ref, wc_ref, g_ref, win_ref, wout_ref, o_ref, u_ref = refs
        t = _dot(a_ref[...], wc_ref[...])
    else:
        (x_ref, bg_ref, cur_ref, prev_ref, next_ref, mod_ref, dw_ref, wc_ref, g_ref, win_ref, wout_ref,
         o_ref, u_ref, cwin_ref) = refs
        conv = _dwconv(cwin_ref, None, cur_ref, prev_ref, next_ref, dw_ref, halo)
        t = _dot((bg_ref[...] * conv).astype(BF16), wc_ref[...])
    x = x_ref[...] + mod_ref[mod_base + 2:mod_base + 3, :] * t
    o_ref[...] = _swiglu_half_step(x, mod_ref, mod_base + 3, g_ref, win_ref, wout_ref, u_ref)


def _ffn(x, mods, g, w_in, w_out, mod_base):
    b, n, d = x.shape
    f = w_out.shape[0]
    assert f % MXU_DIM == 0
    tm = _ffn_tile(n)
    kern = functools.partial(_ffn_kernel, mod_base=mod_base)
    return pl.pallas_call(
        kern,
        grid=(b, n // tm),
        in_specs=[_tok_spec(tm, d), _mod_spec(mods), _const_spec(g), _const_spec(w_in), _const_spec(w_out)],
        out_specs=_tok_spec(tm, d),
        out_shape=jax.ShapeDtypeStruct(x.shape, F32),
        scratch_shapes=[pltpu.VMEM((tm, f), BF16)],
        compiler_params=_params(2),
        name="ffn",
    )(x, mods, g, w_in, w_out)


def _tail_ffn(kind, x, tok_ins, conv_in, mods, consts, g, w_in, w_out, mod_base):
    b, n, d = x.shape
    f = w_out.shape[0]
    assert f % MXU_DIM == 0
    tm = _ffn_tile(n) if kind == "attn" else _token_tile(n)
    ins = [x] + list(tok_ins)
    specs = [_tok_spec(tm, d)] * len(ins)
    scratch = [pltpu.VMEM((tm, f), BF16)]
    halo = None
    if conv_in is not None:
        assert consts[0].shape[0] <= SUBLANES
        halo = _halo_width(consts[0].shape[0])
        ins += [conv_in] * 3
        specs += [_tok_spec(tm, d), *_halo_specs(tm, d, halo, n)]
        scratch.append(pltpu.VMEM((tm + 2 * halo, d), F32))
    consts = list(consts) + [g, w_in, w_out]
    kern = functools.partial(_tail_ffn_kernel, kind=kind, mod_base=mod_base, halo=halo)
    return pl.pallas_call(
        kern,
        grid=(b, n // tm),
        in_specs=specs + [_mod_spec(mods)] + [_const_spec(a) for a in consts],
        out_specs=_tok_spec(tm, d),
        out_shape=jax.ShapeDtypeStruct(x.shape, F32),
        scratch_shapes=scratch,
        compiler_params=_params(2),
        name=kind + "_tail_ffn",
    )(*ins, mods, *consts)


def _qkv_kernel(*refs, mod_base, use_rope):
    if use_rope:
        (x_ref, mod_ref, g_ref, w_ref, gm_ref, qg_ref, kg_ref, cos_ref, sin_ref,
         q_ref, k_ref, vt_ref) = refs
    else:
        x_ref, mod_ref, g_ref, w_ref, gm_ref, qg_ref, kg_ref, q_ref, k_ref, vt_ref = refs
    x = x_ref[...]
    d = x.shape[1]
    shift, scale, _ = _mod_rows(mod_ref, mod_base)
    h = _ada_in(x, g_ref[...], shift, scale).astype(BF16)
    if use_rope:
        cos = cos_ref[...]
        sin = sin_ref[...]
        lane = lax.broadcasted_iota(jnp.int32, cos.shape, 1)
        low_half = (lane % 32) < 16

    def norm_rope(col0, gain_ref, out_ref, post_scale):
        for cc in range(0, d, PROJ_CHUNK):
            t = _dot(h, w_ref[:, col0 + cc:col0 + cc + PROJ_CHUNK])
            for p in range(0, PROJ_CHUNK, MXU_DIM):
                sl = t[:, p:p + MXU_DIM]
                ms = _dot((sl * sl).astype(BF16), gm_ref[...])
                sl = sl * lax.rsqrt(ms + EPS) * gain_ref[...]
                for hh in range(0, MXU_DIM, LANES):
                    y = sl[:, hh:hh + LANES]
                    if use_rope:
                        rot = jnp.where(low_half, pltpu.roll(y, LANES - 16, 1), pltpu.roll(y, 16, 1))
                        y = y * cos + rot * sin
                    c0 = cc + p + hh
                    out_ref[:, c0:c0 + LANES] = (y * post_scale).astype(BF16)

    head_dim = d // (2 * N_HEADS)
    norm_rope(0, qg_ref, q_ref, head_dim ** -0.5 * LOG2E)
    norm_rope(d, kg_ref, k_ref, 1.0)
    for cc in range(0, d, PROJ_CHUNK):
        v = _dot(h, w_ref[:, 2 * d + cc:2 * d + cc + PROJ_CHUNK])
        vt_ref[cc:cc + PROJ_CHUNK, :] = v.T.astype(BF16)


def _qkv_proj(x, mods, g, w_qkv, gmat, qg, kg, rope, mod_base):
    b, n, d = x.shape
    tm = _token_tile(n)
    use_rope = rope is not None
    ins = [x, mods, g, w_qkv, gmat, qg, kg]
    specs = [_tok_spec(tm, d), _mod_spec(mods), _const_spec(g), _const_spec(w_qkv), _const_spec(gmat),
             _const_spec(qg), _const_spec(kg)]
    if use_rope:
        ins += list(rope)
        specs += [pl.BlockSpec((tm, LANES), lambda bb, i: (i, 0))] * 2
    kern = functools.partial(_qkv_kernel, mod_base=mod_base, use_rope=use_rope)
    return pl.pallas_call(
        kern,
        grid=(b, n // tm),
        in_specs=specs,
        out_specs=[_tok_spec(tm, d), _tok_spec(tm, d),
                   pl.BlockSpec((None, d, tm), lambda bb, i: (bb, 0, i))],
        out_shape=[jax.ShapeDtypeStruct((b, n, d), BF16), jax.ShapeDtypeStruct((b, n, d), BF16),
                   jax.ShapeDtypeStruct((b, d, n), BF16)],
        compiler_params=_params(2),
        name="qkv_proj",
    )(*ins)


def _flash_kernel(*refs, n_seg, tk, lam_init, online):
    shift_ref, lamp_ref, subg_ref, q_ref = refs[:4]
    seg_refs = refs[4:4 + 2 * n_seg]
    o_ref = refs[4 + 2 * n_seg]

    lp = lamp_ref[...]
    lam = (jnp.exp(jnp.sum(lp[0:1] * lp[1:2], keepdims=True))
           - jnp.exp(jnp.sum(lp[2:3] * lp[3:4], keepdims=True)) + lam_init)

    q = q_ref[...]
    tq, vd = q.shape
    lane = lax.broadcasted_iota(jnp.int32, q.shape, 1)
    zero = jnp.zeros_like(q)
    qm = (jnp.where(lane < vd // 2, q, zero), jnp.where(lane >= vd // 2, q, zero))

    if online:
        m = [jnp.full((1, tq), -jnp.inf, F32) for _ in range(2)]
        l = [jnp.zeros((1, tq), F32) for _ in range(2)]
        acc = [jnp.zeros((vd, tq), F32) for _ in range(2)]
    else:
        shift = shift_ref[0]
        q2 = jnp.concatenate(qm, axis=0)
        l2 = jnp.zeros((8, 2 * tq), F32)
        acc2 = jnp.zeros((vd, 2 * tq), F32)
    for s in range(n_seg):
        k_ref, vt_ref = seg_refs[2 * s], seg_refs[2 * s + 1]
        nk = k_ref.shape[0]
        ck = min(tk, nk)
        for c in range(nk // ck):
            kch = k_ref[c * ck:(c + 1) * ck, :]
            vtch = vt_ref[:, c * ck:(c + 1) * ck]
            if online:
                for j in range(2):
                    st = _dot_nt(kch, qm[j])
                    m_new = jnp.maximum(m[j], jnp.max(st, axis=0, keepdims=True))
                    alpha = jnp.exp2(m[j] - m_new)
                    p = jnp.exp2(st - m_new)
                    l[j] = alpha * l[j] + jnp.sum(p, axis=0, keepdims=True)
                    acc[j] = alpha * acc[j] + _dot(vtch, p.astype(BF16))
                    m[j] = m_new
            else:
                p = jnp.exp2(_dot_nt(kch, q2) - shift)
                l2 = l2 + jnp.sum(p.reshape(ck // 8, 8, 2 * tq), axis=0)
                acc2 = acc2 + _dot(vtch, p.astype(BF16))

    if not online:
        l2 = jnp.sum(l2, axis=0, keepdims=True)
        l = [l2[:, :tq], l2[:, tq:]]
        acc = [acc2[:, :tq], acc2[:, tq:]]
    ot = acc[0] / l[0] - lam * (acc[1] / l[1])
    ot = ot * lax.rsqrt(jnp.mean(ot * ot, axis=0, keepdims=True) + EPS)
    ot = ot * subg_ref[...] * (1.0 - lam_init)
    o_ref[...] = ot.T.astype(BF16)


def _flash(q, segs, lam_p, subg, score_bound, lam_init):
    b, lq, d = q.shape
    vd = d // N_HEADS
    tq = min(FLASH_TQ, lq)
    ins = [score_bound.reshape(1), lam_p, subg, q]
    specs = [pl.BlockSpec(memory_space=pltpu.SMEM),
             pl.BlockSpec(lam_p.shape, lambda bb, h, i: (0, 0)),
             pl.BlockSpec(subg.shape, lambda bb, h, i: (0, 0)),
             pl.BlockSpec((None, tq, vd), lambda bb, h, i: (bb, i, h))]
    for k, vt in segs:
        nk = k.shape[1]
        ins += [k, vt]
        specs += [pl.BlockSpec((None, nk, vd), lambda bb, h, i: (bb, 0, h)),
                  pl.BlockSpec((None, vd, nk), lambda bb, h, i: (bb, h, 0))]

    def call(online):
        kern = functools.partial(_flash_kernel, n_seg=len(segs), tk=FLASH_TK, lam_init=lam_init, online=online)
        return pl.pallas_call(
            kern,
            grid=(b, N_HEADS, lq // tq),
            in_specs=specs,
            out_specs=pl.BlockSpec((None, tq, vd), lambda bb, h, i: (bb, i, h)),
            out_shape=jax.ShapeDtypeStruct((b, lq, d), BF16),
            compiler_params=_params(3),
            name="diff_flash_online" if online else "diff_flash",
        )

    return lax.cond(score_bound <= MAX_FIXED_SHIFT, lambda *a: call(False)(*a), lambda *a: call(True)(*a), *ins)


def _conf_in_kernel(x_ref, mod_ref, g_ref, w_ref, b_ref, u_ref, *, mod_base):
    x = x_ref[...]
    d = x.shape[1]
    shift, scale, _ = _mod_rows(mod_ref, mod_base)
    h = _ada_in(x, g_ref[...], shift, scale).astype(BF16)
    for c in range(0, d, PROJ_CHUNK):
        a = _dot(h, w_ref[:, c:c + PROJ_CHUNK]) + b_ref[:, c:c + PROJ_CHUNK]
        gt = _dot(h, w_ref[:, d + c:d + c + PROJ_CHUNK]) + b_ref[:, d + c:d + c + PROJ_CHUNK]
        u_ref[:, c:c + PROJ_CHUNK] = a * _sigmoid(gt)


def _conf_in(x, mods, g, w_in, b_in, mod_base):
    b, n, d = x.shape
    tm = _token_tile(n)
    kern = functools.partial(_conf_in_kernel, mod_base=mod_base)
    return pl.pallas_call(
        kern,
        grid=(b, n // tm),
        in_specs=[_tok_spec(tm, d), _mod_spec(mods), _const_spec(g), _const_spec(w_in), _const_spec(b_in)],
        out_specs=_tok_spec(tm, d),
        out_shape=jax.ShapeDtypeStruct(x.shape, F32),
        compiler_params=_params(2),
        name="conf_in",
    )(x, mods, g, w_in, b_in)


def _sc_in_kernel(x_ref, mod_ref, g_ref, w_ref, b_out_ref, m_out_ref, *, mod_base):
    x = x_ref[...]
    d = x.shape[1]
    shift, scale, _ = _mod_rows(mod_ref, mod_base)
    h = _ada_in(x, g_ref[...], shift, scale).astype(BF16)
    for c in range(0, d, PROJ_CHUNK):
        b_out_ref[:, c:c + PROJ_CHUNK] = _dot(h, w_ref[:, c:c + PROJ_CHUNK])
        cg = _dot(h, w_ref[:, d + c:d + c + PROJ_CHUNK])
        xh = _dot(h, w_ref[:, 2 * d + c:2 * d + c + PROJ_CHUNK])
        m_out_ref[:, c:c + PROJ_CHUNK] = cg * xh


def _sc_in(x, mods, g, w_in, mod_base):
    b, n, d = x.shape
    tm = _token_tile(n)
    kern = functools.partial(_sc_in_kernel, mod_base=mod_base)
    return pl.pallas_call(
        kern,
        grid=(b, n // tm),
        in_specs=[_tok_spec(tm, d), _mod_spec(mods), _const_spec(g), _const_spec(w_in)],
        out_specs=[_tok_spec(tm, d), _tok_spec(tm, d)],
        out_shape=[jax.ShapeDtypeStruct(x.shape, F32)] * 2,
        compiler_params=_params(2),
        name="sconv_in",
    )(x, mods, g, w_in)


def _fill_conv_window(cwin_ref, cur_ref, prev_ref, next_ref, halo):
    tm = cur_ref.shape[0]
    i = pl.program_id(1)
    cwin_ref[0:halo] = jnp.where(i == 0, 0.0, prev_ref[...])
    cwin_ref[halo:halo + tm] = cur_ref[...]
    cwin_ref[halo + tm:] = jnp.where(i == pl.num_programs(1) - 1, 0.0, next_ref[...])


def _dwconv(cwin_ref, al_ref, cur_ref, prev_ref, next_ref, dw_ref, halo):
    tm = cur_ref.shape[0]
    taps = dw_ref.shape[0]
    _fill_conv_window(cwin_ref, cur_ref, prev_ref, next_ref, halo)
    off = halo - taps // 2
    acc = None
    for s in range(min(SUBLANES, taps)):
        ks = range(s, taps, SUBLANES)
        src, base = cwin_ref, off + s
        if len(ks) > 1 and base % SUBLANES:
            span = tm + SUBLANES * (len(ks) - 1)
            al_ref[0:span, :] = cwin_ref[base:base + span, :]
            src, base = al_ref, 0
        for jj, k in enumerate(ks):
            term = src[base + SUBLANES * jj:base + SUBLANES * jj + tm, :] * dw_ref[k:k + 1, :]
            acc = term if acc is None else acc + term
    return acc


def _conf_out_kernel(x_ref, mod_ref, cur_ref, prev_ref, next_ref, dw_ref, dwb_ref, lng_ref, lnb_ref,
                     w_ref, b_ref, o_ref, cwin_ref, al_ref, *, mod_base, halo):
    u = _dwconv(cwin_ref, al_ref, cur_ref, prev_ref, next_ref, dw_ref, halo) + dwb_ref[...]
    uc = u - jnp.mean(u, axis=-1, keepdims=True)
    var = jnp.mean(uc * uc, axis=-1, keepdims=True)
    y = uc * lax.rsqrt(var + EPS) * lng_ref[...] + lnb_ref[...]
    y = (y * _sigmoid(y)).astype(BF16)
    gate = mod_ref[mod_base + 2:mod_base + 3, :]
    o_ref[...] = x_ref[...] + gate * (_dot(y, w_ref[...]) + b_ref[...])


def _halo_width(taps):
    return SUBLANES * (-(-(taps // 2) // SUBLANES))


def _halo_specs(tm, d, halo, n):
    r = tm // halo
    nb = n // halo
    prev = pl.BlockSpec((None, halo, d), lambda b, i: (b, jnp.maximum(i * r - 1, 0), 0))
    nxt = pl.BlockSpec((None, halo, d), lambda b, i: (b, jnp.minimum((i + 1) * r, nb - 1), 0))
    return prev, nxt


def _conf_out(x, u, mods, dw_w, dw_b, ln_g, ln_b, w_out, b_out, mod_base):
    b, n, d = x.shape
    tm = _token_tile(n)
    taps = dw_w.shape[0]
    halo = _halo_width(taps)
    prev, nxt = _halo_specs(tm, d, halo, n)
    kern = functools.partial(_conf_out_kernel, mod_base=mod_base, halo=halo)
    consts = [dw_w, dw_b, ln_g, ln_b, w_out, b_out]
    return pl.pallas_call(
        kern,
        grid=(b, n // tm),
        in_specs=[_tok_spec(tm, d), _mod_spec(mods), _tok_spec(tm, d), prev, nxt] + [_const_spec(a) for a in consts],
        out_specs=_tok_spec(tm, d),
        out_shape=jax.ShapeDtypeStruct(x.shape, F32),
        scratch_shapes=[pltpu.VMEM((tm + 2 * halo, d), F32),
                        pltpu.VMEM((tm + SUBLANES * ((taps - 1) // SUBLANES), d), F32)],
        compiler_params=_params(2),
        name="conf_out",
    )(x, mods, u, u, u, *consts)


def _rope_tables(n_tokens, head_dim):
    rows = n_tokens // GRID_W
    row_ids = jnp.repeat(jnp.arange(rows, dtype=F32), GRID_W)
    col_ids = jnp.tile(jnp.arange(GRID_W, dtype=F32), rows)
    half = head_dim // 2
    inv_freq = ROPE_THETA ** (-jnp.arange(0, half, 2, dtype=F32) / half)
    ang_r = row_ids[:, None] * inv_freq
    ang_c = col_ids[:, None] * inv_freq
    ang = jnp.concatenate([ang_r, ang_r, ang_c, ang_c], axis=-1)
    quarter = head_dim // 4
    sign = jnp.where((jnp.arange(head_dim) % (2 * quarter)) < quarter, -1.0, 1.0).astype(F32)
    reps = LANES // head_dim
    return jnp.tile(jnp.cos(ang), (1, reps)), jnp.tile(jnp.sin(ang) * sign, (1, reps))


def _group_mean_matrix(head_dim):
    idx = jnp.arange(MXU_DIM) // head_dim
    return jnp.where(idx[:, None] == idx[None, :], 1.0 / head_dim, 0.0).astype(BF16)


def kernel(x, c, ctx, c_ctx, ada_w, ada_b, norm_g, ffn_w_in, ffn_w_out, attn_w_qkv, attn_w_o, attn_q_g,
           attn_k_g, attn_lambda, attn_subln_g, conv_w_in, conv_b_in, conv_dw_w, conv_dw_b, conv_ln_g,
           conv_ln_b, conv_w_out, conv_b_out, sc_w_in, sc_dw_w, sc_w_out):
    bsz, seq, d = x.shape
    depth = ada_w.shape[0]
    head_dim = d // (2 * N_HEADS)
    row = lambda v: v.reshape(1, -1)

    n_rows = ((bsz + 1 + 7) // 8) * 8
    cvec = jnp.zeros((n_rows, d), F32).at[:bsz].set(c).at[bsz].set(c_ctx)
    mods = _modulations(cvec, ada_w, ada_b).reshape(depth, n_rows, N_MOD, d)

    rope = _rope_tables(seq, head_dim)
    gmat = _group_mean_matrix(head_dim)

    for i in range(depth):
        kind = i % N_MIXERS
        j = i // N_MIXERS
        last = i == depth - 1
        ctx_in = (not last) or kind == 0
        ctx_out = not last
        ml = mods[i, :bsz]
        mc = mods[i, bsz:bsz + 1]
        g = [row(norm_g[i, s]) for s in range(3)]
        w_in = [ffn_w_in[i, s].astype(BF16) for s in range(2)]
        w_out = [ffn_w_out[i, s].astype(BF16) for s in range(2)]

        x = _ffn(x, ml, g[0], w_in[0], w_out[0], 0)
        if ctx_in:
            ctx = _ffn(ctx, mc, g[0], w_in[0], w_out[0], 0)

        if kind == 0:
            lam_init = 0.8 - 0.6 * math.exp(-0.3 * i)
            w_qkv = attn_w_qkv[j].astype(BF16)
            w_o = attn_w_o[j].astype(BF16)
            qg = jnp.tile(attn_q_g[j], MXU_DIM // head_dim).reshape(1, MXU_DIM)
            kg = jnp.tile(attn_k_g[j], MXU_DIM // head_dim).reshape(1, MXU_DIM)
            subg = attn_subln_g[j].reshape(-1, 1)
            bound = (1.02 * LOG2E * head_dim ** 0.5) * jnp.max(jnp.abs(attn_q_g[j])) * jnp.max(jnp.abs(attn_k_g[j]))
            q_l, k_l, vt_l = _qkv_proj(x, ml, g[1], w_qkv, gmat, qg, kg, rope, 3)
            q_c, k_c, vt_c = _qkv_proj(ctx, mc, g[1], w_qkv, gmat, qg, kg, None, 3)
            o_l = _flash(q_l, [(k_l, vt_l), (k_c, vt_c)], attn_lambda[j], subg, bound, lam_init)
            tail = lambda s, o, md: _tail_ffn("attn", s, [o], None, md, (w_o,), g[2], w_in[1], w_out[1], 3)
            x = tail(x, o_l, ml)
            if ctx_out:
                o_c = _flash(q_c, [(k_c, vt_c)], attn_lambda[j], subg, bound, lam_init)
                ctx = tail(ctx, o_c, mc)
        elif kind == 1:
            cw_in = conv_w_in[j].astype(BF16)
            cargs = (conv_dw_w[j], row(conv_dw_b[j]), row(conv_ln_g[j]), row(conv_ln_b[j]),
                     conv_w_out[j].astype(BF16), row(conv_b_out[j]))

            def mixer(s, md):
                u = _conf_in(s, md, g[1], cw_in, row(conv_b_in[j]), 3)
                return _ffn(_conf_out(s, u, md, *cargs, 3), md, g[2], w_in[1], w_out[1], 6)

            x = mixer(x, ml)
            if ctx_out:
                ctx = mixer(ctx, mc)
        else:
            sw_in = sc_w_in[j].astype(BF16)
            tconsts = (sc_dw_w[j], sc_w_out[j].astype(BF16))

            def mixer(s, md):
                bg, m = _sc_in(s, md, g[1], sw_in, 3)
                return _tail_ffn("sconv", s, [bg], m, md, tconsts, g[2], w_in[1], w_out[1], 3)

            x = mixer(x, ml)
            if ctx_out:
                ctx = mixer(ctx, mc)

    return x
```

```python
import functools
import math

import jax
import jax.numpy as jnp
from jax import lax
from jax.experimental import pallas as pl
from jax.experimental.pallas import tpu as pltpu

GRID_W = 64
N_MIXERS = 3
N_HEADS = 8
N_MOD = 9
ROPE_THETA = 10000.0
EPS = 1e-6

LANES = 128
SUBLANES = 8
MXU_DIM = 256
VMEM_LIMIT = 56 * 1024 * 1024
TOKEN_TILE = 512
FFN_TILE = 1024
FLASH_TQ = 1024
FLASH_TK = 2048
PROJ_CHUNK = 2 * MXU_DIM
MAX_FIXED_SHIFT = 60.0
LOG2E = math.log2(math.e)

BF16 = jnp.bfloat16
F32 = jnp.float32


def _dot(a, b):
    return jnp.dot(a, b, preferred_element_type=F32)


def _dot_nt(a, b):
    return lax.dot_general(a, b, (((1,), (1,)), ((), ())), preferred_element_type=F32)


def _sigmoid(x):
    return 1.0 / (1.0 + jnp.exp(-x))


def _ada_in(x, g, shift, scale):
    y = x * lax.rsqrt(jnp.mean(x * x, axis=-1, keepdims=True) + EPS)
    return (y * g) * (1.0 + scale) + shift


def _mod_rows(mod_ref, base):
    return (mod_ref[base:base + 1, :], mod_ref[base + 1:base + 2, :], mod_ref[base + 2:base + 3, :])


def _tok_spec(tm, d):
    return pl.BlockSpec((None, tm, d), lambda b, i: (b, i, 0))


def _mod_spec(mods):
    per_batch = mods.shape[0] > 1
    return pl.BlockSpec((None,) + mods.shape[1:], lambda b, i: (b if per_batch else 0, 0, 0))


def _const_spec(arr):
    nd = arr.ndim
    return pl.BlockSpec(arr.shape, lambda *_: (0,) * nd, pipeline_mode=pl.Buffered(1))


def _params(n_grid):
    return pltpu.CompilerParams(dimension_semantics=("parallel",) * n_grid,
                                vmem_limit_bytes=VMEM_LIMIT)


def _token_tile(n):
    return TOKEN_TILE if n % TOKEN_TILE == 0 else n


def _ffn_tile(n):
    return FFN_TILE if n % FFN_TILE == 0 else n


def _mods_kernel(c_ref, w_ref, b_ref, o_ref):
    c = c_ref[...]
    sc = (c * _sigmoid(c)).astype(BF16)
    o_ref[...] = _dot(sc, w_ref[...].astype(BF16)) + b_ref[...]


def _modulations(cvec, ada_w, ada_b):
    depth, d, n = ada_w.shape
    r = cvec.shape[0]
    tn = n // 8
    return pl.pallas_call(
        _mods_kernel,
        grid=(depth, n // tn),
        in_specs=[pl.BlockSpec((r, d), lambda l, j: (0, 0)),
                  pl.BlockSpec((None, d, tn), lambda l, j: (l, 0, j)),
                  pl.BlockSpec((None, 1, tn), lambda l, j: (l, 0, j))],
        out_specs=pl.BlockSpec((None, r, tn), lambda l, j: (l, 0, j)),
        out_shape=jax.ShapeDtypeStruct((depth, r, n), F32),
        compiler_params=_params(2),
        name="adaln_mods",
    )(cvec, ada_w, ada_b.reshape(depth, 1, n))


def _swiglu_half_step(x, mod_ref, mod_base, g_ref, win_ref, wout_ref, u_ref):
    shift, scale, gate = _mod_rows(mod_ref, mod_base)
    h = _ada_in(x, g_ref[...], shift, scale).astype(BF16)
    f = wout_ref.shape[0]
    for c in range(0, f, MXU_DIM):
        a = _dot(h, win_ref[:, c:c + MXU_DIM])
        gt = _dot(h, win_ref[:, f + c:f + c + MXU_DIM])
        u_ref[:, c:c + MXU_DIM] = ((gt * _sigmoid(gt)) * a).astype(BF16)
    return x + (0.5 * gate) * _dot(u_ref[...], wout_ref[...])


def _ffn_kernel(x_ref, mod_ref, g_ref, win_ref, wout_ref, o_ref, u_ref, *, mod_base):
    o_ref[...] = _swiglu_half_step(x_ref[...], mod_ref, mod_base, g_ref, win_ref, wout_ref, u_ref)


def _tail_ffn_kernel(*refs, kind, mod_base, halo):
    if kind == "attn":
        x_ref, a_ref, mod_ref, wc_ref, g_ref, win_ref, wout_ref, o_ref, u_ref = refs
        t = _dot(a_ref[...], wc_ref[...])
    elif kind == "conformer":
        (x_ref, cur_ref, prev_ref, next_ref, mod_ref, dw_ref, dwb_ref, lng_ref, lnb_ref, wc_ref, bc_ref,
         g_ref, win_ref, wout_ref, o_ref, u_ref, cwin_ref, al_ref) = refs
        u = _dwconv(cwin_ref, al_ref, cur_ref, prev_ref, next_ref, dw_ref, halo) + dwb_ref[...]
        uc = u - jnp.mean(u, axis=-1, keepdims=True)
        var = jnp.mean(uc * uc, axis=-1, keepdims=True)
        y = uc * lax.rsqrt(var + EPS) * lng_ref[...] + lnb_ref[...]
        t = _dot((y * _sigmoid(y)).astype(BF16), wc_ref[...]) + bc_ref[...]
    else:
        (x_ref, bg_ref, cur_ref, prev_ref, next_ref, mod_ref, dw_ref, wc_ref, g_ref, win_ref, wout_ref,
         o_ref, u_ref, cwin_ref) = refs
        conv = _dwconv(cwin_ref, None, cur_ref, prev_ref, next_ref, dw_ref, halo)
        t = _dot((bg_ref[...] * conv).astype(BF16), wc_ref[...])
    x = x_ref[...] + mod_ref[mod_base + 2:mod_base + 3, :] * t
    o_ref[...] = _swiglu_half_step(x, mod_ref, mod_base + 3, g_ref, win_ref, wout_ref, u_ref)


def _joint_tokens(mods, *arrays):
    if mods.shape[0] == 1:
        return [a.reshape(1, -1, a.shape[-1]) for a in arrays]
    return list(arrays)


def _ffn(x, mods, g, w_in, w_out, mod_base):
    shape = x.shape
    x, = _joint_tokens(mods, x)
    b, n, d = x.shape
    f = w_out.shape[0]
    assert f % MXU_DIM == 0
    tm = _ffn_tile(n)
    kern = functools.partial(_ffn_kernel, mod_base=mod_base)
    return pl.pallas_call(
        kern,
        grid=(b, n // tm),
        in_specs=[_tok_spec(tm, d), _mod_spec(mods), _const_spec(g), _const_spec(w_in), _const_spec(w_out)],
        out_specs=_tok_spec(tm, d),
        out_shape=jax.ShapeDtypeStruct(x.shape, F32),
        scratch_shapes=[pltpu.VMEM((tm, f), BF16)],
        compiler_params=_params(2),
        name="ffn",
    )(x, mods, g, w_in, w_out).reshape(shape)


def _tail_ffn(kind, x, tok_ins, conv_in, mods, consts, g, w_in, w_out, mod_base):
    shape = x.shape
    if conv_in is None:
        x, *tok_ins = _joint_tokens(mods, x, *tok_ins)
    b, n, d = x.shape
    f = w_out.shape[0]
    assert f % MXU_DIM == 0
    tm = _ffn_tile(n) if kind == "attn" else _token_tile(n)
    ins = [x] + list(tok_ins)
    specs = [_tok_spec(tm, d)] * len(ins)
    scratch = [pltpu.VMEM((tm, f), BF16)]
    halo = None
    if conv_in is not None:
        taps = consts[0].shape[0]
        halo = _halo_width(taps)
        ins += [conv_in] * 3
        specs += [_tok_spec(tm, d), *_halo_specs(tm, d, halo, n)]
        scratch.append(pltpu.VMEM((tm + 2 * halo, d), F32))
        if taps > SUBLANES:
            scratch.append(pltpu.VMEM((tm + SUBLANES * ((taps - 1) // SUBLANES), d), F32))
    consts = list(consts) + [g, w_in, w_out]
    kern = functools.partial(_tail_ffn_kernel, kind=kind, mod_base=mod_base, halo=halo)
    return pl.pallas_call(
        kern,
        grid=(b, n // tm),
        in_specs=specs + [_mod_spec(mods)] + [_const_spec(a) for a in consts],
        out_specs=_tok_spec(tm, d),
        out_shape=jax.ShapeDtypeStruct(x.shape, F32),
        scratch_shapes=scratch,
        compiler_params=_params(2),
        name=kind + "_tail_ffn",
    )(*ins, mods, *consts).reshape(shape)


def _qkv_kernel(*refs, mod_base, use_rope):
    if use_rope:
        (x_ref, mod_ref, g_ref, w_ref, gm_ref, qg_ref, kg_ref, cos_ref, sin_ref,
         q_ref, k_ref, vt_ref) = refs
    else:
        x_ref, mod_ref, g_ref, w_ref, gm_ref, qg_ref, kg_ref, q_ref, k_ref, vt_ref = refs
    x = x_ref[...]
    d = x.shape[1]
    shift, scale, _ = _mod_rows(mod_ref, mod_base)
    h = _ada_in(x, g_ref[...], shift, scale).astype(BF16)
    if use_rope:
        cos = cos_ref[...]
        sin = sin_ref[...]
        lane = lax.broadcasted_iota(jnp.int32, cos.shape, 1)
        low_half = (lane % 32) < 16

    def norm_rope(col0, gain_ref, out_ref, post_scale):
        for cc in range(0, d, PROJ_CHUNK):
            t = _dot(h, w_ref[:, col0 + cc:col0 + cc + PROJ_CHUNK])
            for p in range(0, PROJ_CHUNK, MXU_DIM):
                sl = t[:, p:p + MXU_DIM]
                ms = _dot((sl * sl).astype(BF16), gm_ref[...])
                sl = sl * lax.rsqrt(ms + EPS) * gain_ref[...]
                for hh in range(0, MXU_DIM, LANES):
                    y = sl[:, hh:hh + LANES]
                    if use_rope:
                        rot = jnp.where(low_half, pltpu.roll(y, LANES - 16, 1), pltpu.roll(y, 16, 1))
                        y = y * cos + rot * sin
                    c0 = cc + p + hh
                    out_ref[:, c0:c0 + LANES] = (y * post_scale).astype(BF16)

    head_dim = d // (2 * N_HEADS)
    norm_rope(0, qg_ref, q_ref, head_dim ** -0.5 * LOG2E)
    norm_rope(d, kg_ref, k_ref, 1.0)
    for cc in range(0, d, PROJ_CHUNK):
        v = _dot(h, w_ref[:, 2 * d + cc:2 * d + cc + PROJ_CHUNK])
        vt_ref[cc:cc + PROJ_CHUNK, :] = v.T.astype(BF16)


def _qkv_proj(x, mods, g, w_qkv, gmat, qg, kg, rope, mod_base):
    b, n, d = x.shape
    tm = _ffn_tile(n)
    use_rope = rope is not None
    ins = [x, mods, g, w_qkv, gmat, qg, kg]
    specs = [_tok_spec(tm, d), _mod_spec(mods), _const_spec(g), _const_spec(w_qkv), _const_spec(gmat),
             _const_spec(qg), _const_spec(kg)]
    if use_rope:
        ins += list(rope)
        specs += [pl.BlockSpec((tm, LANES), lambda bb, i: (i, 0))] * 2
    kern = functools.partial(_qkv_kernel, mod_base=mod_base, use_rope=use_rope)
    return pl.pallas_call(
        kern,
        grid=(b, n // tm),
        in_specs=specs,
        out_specs=[_tok_spec(tm, d), _tok_spec(tm, d),
                   pl.BlockSpec((None, d, tm), lambda bb, i: (bb, 0, i))],
        out_shape=[jax.ShapeDtypeStruct((b, n, d), BF16), jax.ShapeDtypeStruct((b, n, d), BF16),
                   jax.ShapeDtypeStruct((b, d, n), BF16)],
        compiler_params=_params(2),
        name="qkv_proj",
    )(*ins)


def _flash_kernel(*refs, n_seg, tk, lam_init, online):
    shift_ref, lamp_ref, subg_ref, q_ref = refs[:4]
    seg_refs = refs[4:4 + 2 * n_seg]
    o_ref = refs[4 + 2 * n_seg]

    lp = lamp_ref[...]
    lam = (jnp.exp(jnp.sum(lp[0:1] * lp[1:2], keepdims=True))
           - jnp.exp(jnp.sum(lp[2:3] * lp[3:4], keepdims=True)) + lam_init)

    q = q_ref[...]
    tq, vd = q.shape
    lane = lax.broadcasted_iota(jnp.int32, q.shape, 1)
    zero = jnp.zeros_like(q)
    qm = (jnp.where(lane < vd // 2, q, zero), jnp.where(lane >= vd // 2, q, zero))

    if online:
        m = [jnp.full((1, tq), -jnp.inf, F32) for _ in range(2)]
        l = [jnp.zeros((1, tq), F32) for _ in range(2)]
        acc = [jnp.zeros((vd, tq), F32) for _ in range(2)]
    else:
        shift = shift_ref[0]
        q2 = jnp.concatenate(qm, axis=0)
        l2 = jnp.zeros((8, 2 * tq), F32)
        acc2 = jnp.zeros((vd, 2 * tq), F32)
    for s in range(n_seg):
        k_ref, vt_ref = seg_refs[2 * s], seg_refs[2 * s + 1]
        nk = k_ref.shape[0]
        ck = min(tk, nk)
        for c in range(nk // ck):
            kch = k_ref[c * ck:(c + 1) * ck, :]
            vtch = vt_ref[:, c * ck:(c + 1) * ck]
            if online:
                for j in range(2):
                    st = _dot_nt(kch, qm[j])
                    m_new = jnp.maximum(m[j], jnp.max(st, axis=0, keepdims=True))
                    alpha = jnp.exp2(m[j] - m_new)
                    p = jnp.exp2(st - m_new)
                    l[j] = alpha * l[j] + jnp.sum(p, axis=0, keepdims=True)
                    acc[j] = alpha * acc[j] + _dot(vtch, p.astype(BF16))
                    m[j] = m_new
            else:
                p = jnp.exp2(_dot_nt(kch, q2) - shift)
                l2 = l2 + jnp.sum(p.reshape(ck // 8, 8, 2 * tq), axis=0)
                acc2 = acc2 + _dot(vtch, p.astype(BF16))

    if not online:
        l2 = jnp.sum(l2, axis=0, keepdims=True)
        l = [l2[:, :tq], l2[:, tq:]]
        acc = [acc2[:, :tq], acc2[:, tq:]]
    ot = acc[0] / l[0] - lam * (acc[1] / l[1])
    ot = ot * lax.rsqrt(jnp.mean(ot * ot, axis=0, keepdims=True) + EPS)
    ot = ot * subg_ref[...] * (1.0 - lam_init)
    o_ref[...] = ot.T.astype(BF16)


def _flash(q, segs, lam_p, subg, score_bound, lam_init):
    b, lq, d = q.shape
    vd = d // N_HEADS
    tq = min(FLASH_TQ, lq)
    ins = [score_bound.reshape(1), lam_p, subg, q]
    specs = [pl.BlockSpec(memory_space=pltpu.SMEM),
             pl.BlockSpec(lam_p.shape, lambda bb, h, i: (0, 0)),
             pl.BlockSpec(subg.shape, lambda bb, h, i: (0, 0)),
             pl.BlockSpec((None, tq, vd), lambda bb, h, i: (bb, i, h))]
    for k, vt in segs:
        nk = k.shape[1]
        ins += [k, vt]
        specs += [pl.BlockSpec((None, nk, vd), lambda bb, h, i: (bb, 0, h)),
                  pl.BlockSpec((None, vd, nk), lambda bb, h, i: (bb, h, 0))]

    def call(online):
        kern = functools.partial(_flash_kernel, n_seg=len(segs), tk=FLASH_TK, lam_init=lam_init, online=online)
        return pl.pallas_call(
            kern,
            grid=(b, N_HEADS, lq // tq),
            in_specs=specs,
            out_specs=pl.BlockSpec((None, tq, vd), lambda bb, h, i: (bb, i, h)),
            out_shape=jax.ShapeDtypeStruct((b, lq, d), BF16),
            compiler_params=_params(3),
            name="diff_flash_online" if online else "diff_flash",
        )

    return lax.cond(score_bound <= MAX_FIXED_SHIFT, lambda *a: call(False)(*a), lambda *a: call(True)(*a), *ins)


def _conf_in_kernel(x_ref, mod_ref, g_ref, w_ref, b_ref, u_ref, *, mod_base):
    x = x_ref[...]
    d = x.shape[1]
    shift, scale, _ = _mod_rows(mod_ref, mod_base)
    h = _ada_in(x, g_ref[...], shift, scale).astype(BF16)
    for c in range(0, d, PROJ_CHUNK):
        a = _dot(h, w_ref[:, c:c + PROJ_CHUNK]) + b_ref[:, c:c + PROJ_CHUNK]
        gt = _dot(h, w_ref[:, d + c:d + c + PROJ_CHUNK]) + b_ref[:, d + c:d + c + PROJ_CHUNK]
        u_ref[:, c:c + PROJ_CHUNK] = a * _sigmoid(gt)


def _conf_in(x, mods, g, w_in, b_in, mod_base):
    b, n, d = x.shape
    tm = _token_tile(n)
    kern = functools.partial(_conf_in_kernel, mod_base=mod_base)
    return pl.pallas_call(
        kern,
        grid=(b, n // tm),
        in_specs=[_tok_spec(tm, d), _mod_spec(mods), _const_spec(g), _const_spec(w_in), _const_spec(b_in)],
        out_specs=_tok_spec(tm, d),
        out_shape=jax.ShapeDtypeStruct(x.shape, F32),
        compiler_params=_params(2),
        name="conf_in",
    )(x, mods, g, w_in, b_in)


def _sc_in_kernel(x_ref, mod_ref, g_ref, w_ref, b_out_ref, m_out_ref, *, mod_base):
    x = x_ref[...]
    d = x.shape[1]
    shift, scale, _ = _mod_rows(mod_ref, mod_base)
    h = _ada_in(x, g_ref[...], shift, scale).astype(BF16)
    for c in range(0, d, PROJ_CHUNK):
        b_out_ref[:, c:c + PROJ_CHUNK] = _dot(h, w_ref[:, c:c + PROJ_CHUNK])
        cg = _dot(h, w_ref[:, d + c:d + c + PROJ_CHUNK])
        xh = _dot(h, w_ref[:, 2 * d + c:2 * d + c + PROJ_CHUNK])
        m_out_ref[:, c:c + PROJ_CHUNK] = cg * xh


def _sc_in(x, mods, g, w_in, mod_base):
    b, n, d = x.shape
    tm = _token_tile(n)
    kern = functools.partial(_sc_in_kernel, mod_base=mod_base)
    return pl.pallas_call(
        kern,
        grid=(b, n // tm),
        in_specs=[_tok_spec(tm, d), _mod_spec(mods), _const_spec(g), _const_spec(w_in)],
        out_specs=[_tok_spec(tm, d), _tok_spec(tm, d)],
        out_shape=[jax.ShapeDtypeStruct(x.shape, F32)] * 2,
        compiler_params=_params(2),
        name="sconv_in",
    )(x, mods, g, w_in)


def _fill_conv_window(cwin_ref, cur_ref, prev_ref, next_ref, halo):
    tm = cur_ref.shape[0]
    i = pl.program_id(1)
    cwin_ref[0:halo] = jnp.where(i == 0, 0.0, prev_ref[...])
    cwin_ref[halo:halo + tm] = cur_ref[...]
    cwin_ref[halo + tm:] = jnp.where(i == pl.num_programs(1) - 1, 0.0, next_ref[...])


def _dwconv(cwin_ref, al_ref, cur_ref, prev_ref, next_ref, dw_ref, halo):
    tm = cur_ref.shape[0]
    taps = dw_ref.shape[0]
    _fill_conv_window(cwin_ref, cur_ref, prev_ref, next_ref, halo)
    off = halo - taps // 2
    acc = None
    for s in range(min(SUBLANES, taps)):
        ks = range(s, taps, SUBLANES)
        src, base = cwin_ref, off + s
        if len(ks) > 1 and base % SUBLANES:
            span = tm + SUBLANES * (len(ks) - 1)
            al_ref[0:span, :] = cwin_ref[base:base + span, :]
            src, base = al_ref, 0
        for jj, k in enumerate(ks):
            term = src[base + SUBLANES * jj:base + SUBLANES * jj + tm, :] * dw_ref[k:k + 1, :]
            acc = term if acc is None else acc + term
    return acc


def _halo_width(taps):
    return SUBLANES * (-(-(taps // 2) // SUBLANES))


def _halo_specs(tm, d, halo, n):
    r = tm // halo
    nb = n // halo
    prev = pl.BlockSpec((None, halo, d), lambda b, i: (b, jnp.maximum(i * r - 1, 0), 0))
    nxt = pl.BlockSpec((None, halo, d), lambda b, i: (b, jnp.minimum((i + 1) * r, nb - 1), 0))
    return prev, nxt


def _rope_tables(n_tokens, head_dim):
    rows = n_tokens // GRID_W
    row_ids = jnp.repeat(jnp.arange(rows, dtype=F32), GRID_W)
    col_ids = jnp.tile(jnp.arange(GRID_W, dtype=F32), rows)
    half = head_dim // 2
    inv_freq = ROPE_THETA ** (-jnp.arange(0, half, 2, dtype=F32) / half)
    ang_r = row_ids[:, None] * inv_freq
    ang_c = col_ids[:, None] * inv_freq
    ang = jnp.concatenate([ang_r, ang_r, ang_c, ang_c], axis=-1)
    quarter = head_dim // 4
    sign = jnp.where((jnp.arange(head_dim) % (2 * quarter)) < quarter, -1.0, 1.0).astype(F32)
    reps = LANES // head_dim
    return jnp.tile(jnp.cos(ang), (1, reps)), jnp.tile(jnp.sin(ang) * sign, (1, reps))


def _group_mean_matrix(head_dim):
    idx = jnp.arange(MXU_DIM) // head_dim
    return jnp.where(idx[:, None] == idx[None, :], 1.0 / head_dim, 0.0).astype(BF16)


def kernel(x, c, ctx, c_ctx, ada_w, ada_b, norm_g, ffn_w_in, ffn_w_out, attn_w_qkv, attn_w_o, attn_q_g,
           attn_k_g, attn_lambda, attn_subln_g, conv_w_in, conv_b_in, conv_dw_w, conv_dw_b, conv_ln_g,
           conv_ln_b, conv_w_out, conv_b_out, sc_w_in, sc_dw_w, sc_w_out):
    bsz, seq, d = x.shape
    depth = ada_w.shape[0]
    head_dim = d // (2 * N_HEADS)
    row = lambda v: v.reshape(1, -1)

    n_rows = ((bsz + 1 + 7) // 8) * 8
    cvec = jnp.zeros((n_rows, d), F32).at[:bsz].set(c).at[bsz].set(c_ctx)
    mods = _modulations(cvec, ada_w, ada_b).reshape(depth, n_rows, N_MOD, d)

    rope = _rope_tables(seq, head_dim)
    gmat = _group_mean_matrix(head_dim)

    for i in range(depth):
        kind = i % N_MIXERS
        j = i // N_MIXERS
        last = i == depth - 1
        ctx_in = (not last) or kind == 0
        ctx_out = not last
        ml = mods[i, :bsz]
        mc = mods[i, bsz:bsz + 1]
        g = [row(norm_g[i, s]) for s in range(3)]
        w_in = [ffn_w_in[i, s].astype(BF16) for s in range(2)]
        w_out = [ffn_w_out[i, s].astype(BF16) for s in range(2)]

        x = _ffn(x, ml, g[0], w_in[0], w_out[0], 0)
        if ctx_in:
            ctx = _ffn(ctx, mc, g[0], w_in[0], w_out[0], 0)

        if kind == 0:
            lam_init = 0.8 - 0.6 * math.exp(-0.3 * i)
            w_qkv = attn_w_qkv[j].astype(BF16)
            w_o = attn_w_o[j].astype(BF16)
            qg = jnp.tile(attn_q_g[j], MXU_DIM // head_dim).reshape(1, MXU_DIM)
            kg = jnp.tile(attn_k_g[j], MXU_DIM // head_dim).reshape(1, MXU_DIM)
            subg = attn_subln_g[j].reshape(-1, 1)
            bound = (1.02 * LOG2E * head_dim ** 0.5) * jnp.max(jnp.abs(attn_q_g[j])) * jnp.max(jnp.abs(attn_k_g[j]))
            q_l, k_l, vt_l = _qkv_proj(x, ml, g[1], w_qkv, gmat, qg, kg, rope, 3)
            q_c, k_c, vt_c = _qkv_proj(ctx, mc, g[1], w_qkv, gmat, qg, kg, None, 3)
            o_l = _flash(q_l, [(k_l, vt_l), (k_c, vt_c)], attn_lambda[j], subg, bound, lam_init)
            tail = lambda s, o, md: _tail_ffn("attn", s, [o], None, md, (w_o,), g[2], w_in[1], w_out[1], 3)
            x = tail(x, o_l, ml)
            if ctx_out:
                o_c = _flash(q_c, [(k_c, vt_c)], attn_lambda[j], subg, bound, lam_init)
                ctx = tail(ctx, o_c, mc)
        elif kind == 1:
            cw_in = conv_w_in[j].astype(BF16)
            tconsts = (conv_dw_w[j], row(conv_dw_b[j]), row(conv_ln_g[j]), row(conv_ln_b[j]),
                       conv_w_out[j].astype(BF16), row(conv_b_out[j]))

            def mixer(s, md):
                u = _conf_in(s, md, g[1], cw_in, row(conv_b_in[j]), 3)
                return _tail_ffn("conformer", s, [], u, md, tconsts, g[2], w_in[1], w_out[1], 3)

            x = mixer(x, ml)
            if ctx_out:
                ctx = mixer(ctx, mc)
        else:
            sw_in = sc_w_in[j].astype(BF16)
            tconsts = (sc_dw_w[j], sc_w_out[j].astype(BF16))

            def mixer(s, md):
                bg, m = _sc_in(s, md, g[1], sw_in, 3)
                return _tail_ffn("sconv", s, [bg], m, md, tconsts, g[2], w_in[1], w_out[1], 3)

            x = mixer(x, ml)
            if ctx_out:
                ctx = mixer(ctx, mc)

    return x
```

```python
import functools
import math

import jax
import jax.numpy as jnp
from jax import lax
from jax.experimental import pallas as pl
from jax.experimental.pallas import tpu as pltpu

GRID_W = 64
N_MIXERS = 3
N_HEADS = 8
N_MOD = 9
ROPE_THETA = 10000.0
EPS = 1e-6

LANES = 128
SUBLANES = 8
MXU_DIM = 256
VMEM_LIMIT = 56 * 1024 * 1024
TOKEN_TILE = 512
FFN_TILE = 1024
FLASH_TQ = 1024
FLASH_TK = 2048
PROJ_CHUNK = 2 * MXU_DIM
MAX_FIXED_SHIFT = 60.0
LOG2E = math.log2(math.e)

BF16 = jnp.bfloat16
F32 = jnp.float32


def _dot(a, b):
    return jnp.dot(a, b, preferred_element_type=F32)


def _dot_nt(a, b):
    return lax.dot_general(a, b, (((1,), (1,)), ((), ())), preferred_element_type=F32)


def _sigmoid(x):
    return 1.0 / (1.0 + jnp.exp(-x))


def _ada_in(x, g, shift, scale):
    y = x * lax.rsqrt(jnp.mean(x * x, axis=-1, keepdims=True) + EPS)
    return (y * g) * (1.0 + scale) + shift


def _mod_rows(mod_ref, base):
    return (mod_ref[base:base + 1, :], mod_ref[base + 1:base + 2, :], mod_ref[base + 2:base + 3, :])


def _tok_spec(tm, d):
    return pl.BlockSpec((None, tm, d), lambda b, i: (b, i, 0))


def _mod_spec(mods):
    per_batch = mods.shape[0] > 1
    return pl.BlockSpec((None,) + mods.shape[1:], lambda b, i: (b if per_batch else 0, 0, 0))


def _const_spec(arr):
    idx = ()
    if isinstance(arr, tuple):
        arr, idx = arr
    k = len(idx)
    index = tuple(idx) + (0,) * (arr.ndim - k)
    return pl.BlockSpec((None,) * k + arr.shape[k:], lambda *_: index, pipeline_mode=pl.Buffered(1))


def _operand(arr):
    return arr[0] if isinstance(arr, tuple) else arr


def _slab_shape(arr):
    return arr[0].shape[len(arr[1]):] if isinstance(arr, tuple) else arr.shape


def _params(n_grid):
    return pltpu.CompilerParams(dimension_semantics=("parallel",) * n_grid,
                                vmem_limit_bytes=VMEM_LIMIT)


def _token_tile(n):
    return TOKEN_TILE if n % TOKEN_TILE == 0 else n


def _ffn_tile(n):
    return FFN_TILE if n % FFN_TILE == 0 else n


def _mods_kernel(c_ref, w_ref, b_ref, o_ref):
    c = c_ref[...]
    sc = (c * _sigmoid(c)).astype(BF16)
    o_ref[...] = _dot(sc, w_ref[...].astype(BF16)) + b_ref[...]


def _modulations(cvec, ada_w, ada_b):
    depth, d, n = ada_w.shape
    r = cvec.shape[0]
    tn = n // 8
    return pl.pallas_call(
        _mods_kernel,
        grid=(depth, n // tn),
        in_specs=[pl.BlockSpec((r, d), lambda l, j: (0, 0)),
                  pl.BlockSpec((None, d, tn), lambda l, j: (l, 0, j)),
                  pl.BlockSpec((None, 1, tn), lambda l, j: (l, 0, j))],
        out_specs=pl.BlockSpec((None, r, tn), lambda l, j: (l, 0, j)),
        out_shape=jax.ShapeDtypeStruct((depth, r, n), F32),
        compiler_params=_params(2),
        name="adaln_mods",
    )(cvec, ada_w, ada_b.reshape(depth, 1, n))


def _swiglu_half_step(x, mod_ref, mod_base, g_ref, win_ref, wout_ref, u_ref):
    shift, scale, gate = _mod_rows(mod_ref, mod_base)
    h = _ada_in(x, g_ref[...], shift, scale).astype(BF16)
    f = wout_ref.shape[0]
    for c in range(0, f, MXU_DIM):
        a = _dot(h, win_ref[:, c:c + MXU_DIM])
        gt = _dot(h, win_ref[:, f + c:f + c + MXU_DIM])
        u_ref[:, c:c + MXU_DIM] = ((gt * _sigmoid(gt)) * a).astype(BF16)
    return x + (0.5 * gate) * _dot(u_ref[...], wout_ref[...])


def _ffn_kernel(x_ref, mod_ref, g_ref, win_ref, wout_ref, o_ref, u_ref, *, mod_base):
    o_ref[...] = _swiglu_half_step(x_ref[...], mod_ref, mod_base, g_ref, win_ref, wout_ref, u_ref)


def _tail_ffn_kernel(*refs, kind, mod_base, halo):
    if kind == "attn":
        x_ref, a_ref, mod_ref, wc_ref, g_ref, win_ref, wout_ref, o_ref, u_ref = refs
        t = _dot(a_ref[...], wc_ref[...])
    elif kind == "conformer":
        (x_ref, cur_ref, prev_ref, next_ref, mod_ref, dw_ref, dwb_ref, lng_ref, lnb_ref, wc_ref, bc_ref,
         g_ref, win_ref, wout_ref, o_ref, u_ref, cwin_ref, al_ref) = refs
        u = _dwconv(cwin_ref, al_ref, cur_ref, prev_ref, next_ref, dw_ref, halo) + dwb_ref[...]
        uc = u - jnp.mean(u, axis=-1, keepdims=True)
        var = jnp.mean(uc * uc, axis=-1, keepdims=True)
        y = uc * lax.rsqrt(var + EPS) * lng_ref[...] + lnb_ref[...]
        t = _dot((y * _sigmoid(y)).astype(BF16), wc_ref[...]) + bc_ref[...]
    else:
        (x_ref, bg_ref, cur_ref, prev_ref, next_ref, mod_ref, dw_ref, wc_ref, g_ref, win_ref, wout_ref,
         o_ref, u_ref, cwin_ref) = refs
        conv = _dwconv(cwin_ref, None, cur_ref, prev_ref, next_ref, dw_ref, halo)
        t = _dot((bg_ref[...] * conv).astype(BF16), wc_ref[...])
    x = x_ref[...] + mod_ref[mod_base + 2:mod_base + 3, :] * t
    o_ref[...] = _swiglu_half_step(x, mod_ref, mod_base + 3, g_ref, win_ref, wout_ref, u_ref)


def _joint_tokens(mods, *arrays):
    if mods.shape[0] == 1:
        return [a.reshape(1, -1, a.shape[-1]) for a in arrays]
    return list(arrays)


def _ffn(x, mods, g, w_in, w_out, mod_base):
    shape = x.shape
    x, = _joint_tokens(mods, x)
    b, n, d = x.shape
    f = _slab_shape(w_out)[0]
    assert f % MXU_DIM == 0
    tm = _ffn_tile(n)
    kern = functools.partial(_ffn_kernel, mod_base=mod_base)
    return pl.pallas_call(
        kern,
        grid=(b, n // tm),
        in_specs=[_tok_spec(tm, d), _mod_spec(mods), _const_spec(g), _const_spec(w_in), _const_spec(w_out)],
        out_specs=_tok_spec(tm, d),
        out_shape=jax.ShapeDtypeStruct(x.shape, F32),
        scratch_shapes=[pltpu.VMEM((tm, f), BF16)],
        compiler_params=_params(2),
        name="ffn",
    )(x, mods, g, _operand(w_in), _operand(w_out)).reshape(shape)


def _tail_ffn(kind, x, tok_ins, conv_in, mods, consts, g, w_in, w_out, mod_base):
    shape = x.shape
    if conv_in is None:
        x, *tok_ins = _joint_tokens(mods, x, *tok_ins)
    b, n, d = x.shape
    f = _slab_shape(w_out)[0]
    assert f % MXU_DIM == 0
    tm = _ffn_tile(n) if kind == "attn" else _token_tile(n)
    ins = [x] + list(tok_ins)
    specs = [_tok_spec(tm, d)] * len(ins)
    scratch = [pltpu.VMEM((tm, f), BF16)]
    halo = None
    if conv_in is not None:
        taps = consts[0].shape[0]
        halo = _halo_width(taps)
        ins += [conv_in] * 3
        specs += [_tok_spec(tm, d), *_halo_specs(tm, d, halo, n)]
        scratch.append(pltpu.VMEM((tm + 2 * halo, d), F32))
        if taps > SUBLANES:
            scratch.append(pltpu.VMEM((tm + SUBLANES * ((taps - 1) // SUBLANES), d), F32))
    consts = list(consts) + [g, w_in, w_out]
    kern = functools.partial(_tail_ffn_kernel, kind=kind, mod_base=mod_base, halo=halo)
    return pl.pallas_call(
        kern,
        grid=(b, n // tm),
        in_specs=specs + [_mod_spec(mods)] + [_const_spec(a) for a in consts],
        out_specs=_tok_spec(tm, d),
        out_shape=jax.ShapeDtypeStruct(x.shape, F32),
        scratch_shapes=scratch,
        compiler_params=_params(2),
        name=kind + "_tail_ffn",
    )(*ins, mods, *map(_operand, consts)).reshape(shape)


def _qkv_kernel(*refs, mod_base, use_rope):
    if use_rope:
        (x_ref, mod_ref, g_ref, w_ref, gm_ref, qg_ref, kg_ref, cos_ref, sin_ref,
         q_ref, k_ref, vt_ref) = refs
    else:
        x_ref, mod_ref, g_ref, w_ref, gm_ref, qg_ref, kg_ref, q_ref, k_ref, vt_ref = refs
    x = x_ref[...]
    d = x.shape[1]
    shift, scale, _ = _mod_rows(mod_ref, mod_base)
    h = _ada_in(x, g_ref[...], shift, scale).astype(BF16)
    if use_rope:
        cos = cos_ref[...]
        sin = sin_ref[...]
        lane = lax.broadcasted_iota(jnp.int32, cos.shape, 1)
        low_half = (lane % 32) < 16

    def norm_rope(col0, gain_ref, out_ref, post_scale):
        for cc in range(0, d, PROJ_CHUNK):
            t = _dot(h, w_ref[:, col0 + cc:col0 + cc + PROJ_CHUNK])
            for p in range(0, PROJ_CHUNK, MXU_DIM):
                sl = t[:, p:p + MXU_DIM]
                ms = _dot((sl * sl).astype(BF16), gm_ref[...])
                sl = sl * lax.rsqrt(ms + EPS) * gain_ref[...]
                for hh in range(0, MXU_DIM, LANES):
                    y = sl[:, hh:hh + LANES]
                    if use_rope:
                        rot = jnp.where(low_half, pltpu.roll(y, LANES - 16, 1), pltpu.roll(y, 16, 1))
                        y = y * cos + rot * sin
                    c0 = cc + p + hh
                    out_ref[:, c0:c0 + LANES] = (y * post_scale).astype(BF16)

    head_dim = d // (2 * N_HEADS)
    norm_rope(0, qg_ref, q_ref, head_dim ** -0.5 * LOG2E)
    norm_rope(d, kg_ref, k_ref, 1.0)
    for cc in range(0, d, PROJ_CHUNK):
        v = _dot(h, w_ref[:, 2 * d + cc:2 * d + cc + PROJ_CHUNK])
        vt_ref[cc:cc + PROJ_CHUNK, :] = v.T.astype(BF16)


def _qkv_proj(x, mods, g, w_qkv, gmat, qg, kg, rope, mod_base):
    b, n, d = x.shape
    tm = _ffn_tile(n)
    use_rope = rope is not None
    ins = [x, mods, g, w_qkv, gmat, qg, kg]
    specs = [_tok_spec(tm, d), _mod_spec(mods), _const_spec(g), _const_spec(w_qkv), _const_spec(gmat),
             _const_spec(qg), _const_spec(kg)]
    if use_rope:
        ins += list(rope)
        specs += [pl.BlockSpec((tm, LANES), lambda bb, i: (i, 0))] * 2
    kern = functools.partial(_qkv_kernel, mod_base=mod_base, use_rope=use_rope)
    return pl.pallas_call(
        kern,
        grid=(b, n // tm),
        in_specs=specs,
        out_specs=[_tok_spec(tm, d), _tok_spec(tm, d),
                   pl.BlockSpec((None, d, tm), lambda bb, i: (bb, 0, i))],
        out_shape=[jax.ShapeDtypeStruct((b, n, d), BF16), jax.ShapeDtypeStruct((b, n, d), BF16),
                   jax.ShapeDtypeStruct((b, d, n), BF16)],
        compiler_params=_params(2),
        name="qkv_proj",
    )(*ins)


def _flash_kernel(*refs, n_seg, tk, lam_init, online):
    shift_ref, lamp_ref, subg_ref, q_ref = refs[:4]
    seg_refs = refs[4:4 + 2 * n_seg]
    o_ref = refs[4 + 2 * n_seg]

    lp = lamp_ref[...]
    lam = (jnp.exp(jnp.sum(lp[0:1] * lp[1:2], keepdims=True))
           - jnp.exp(jnp.sum(lp[2:3] * lp[3:4], keepdims=True)) + lam_init)

    q = q_ref[...]
    tq, vd = q.shape
    lane = lax.broadcasted_iota(jnp.int32, q.shape, 1)
    zero = jnp.zeros_like(q)
    qm = (jnp.where(lane < vd // 2, q, zero), jnp.where(lane >= vd // 2, q, zero))

    if online:
        m = [jnp.full((1, tq), -jnp.inf, F32) for _ in range(2)]
        l = [jnp.zeros((1, tq), F32) for _ in range(2)]
        acc = [jnp.zeros((vd, tq), F32) for _ in range(2)]
    else:
        shift = shift_ref[0]
        q2 = jnp.concatenate(qm, axis=0)
        l2 = jnp.zeros((8, 2 * tq), F32)
        acc2 = jnp.zeros((vd, 2 * tq), F32)
    for s in range(n_seg):
        k_ref, vt_ref = seg_refs[2 * s], seg_refs[2 * s + 1]
        nk = k_ref.shape[0]
        ck = min(tk, nk)
        for c in range(nk // ck):
            kch = k_ref[c * ck:(c + 1) * ck, :]
            vtch = vt_ref[:, c * ck:(c + 1) * ck]
            if online:
                for j in range(2):
                    st = _dot_nt(kch, qm[j])
                    m_new = jnp.maximum(m[j], jnp.max(st, axis=0, keepdims=True))
                    alpha = jnp.exp2(m[j] - m_new)
                    p = jnp.exp2(st - m_new)
                    l[j] = alpha * l[j] + jnp.sum(p, axis=0, keepdims=True)
                    acc[j] = alpha * acc[j] + _dot(vtch, p.astype(BF16))
                    m[j] = m_new
            else:
                p = jnp.exp2(_dot_nt(kch, q2) - shift)
                l2 = l2 + jnp.sum(p.reshape(ck // 8, 8, 2 * tq), axis=0)
                acc2 = acc2 + _dot(vtch, p.astype(BF16))

    if not online:
        l2 = jnp.sum(l2, axis=0, keepdims=True)
        l = [l2[:, :tq], l2[:, tq:]]
        acc = [acc2[:, :tq], acc2[:, tq:]]
    ot = acc[0] / l[0] - lam * (acc[1] / l[1])
    ot = ot * lax.rsqrt(jnp.mean(ot * ot, axis=0, keepdims=True) + EPS)
    ot = ot * subg_ref[...] * (1.0 - lam_init)
    o_ref[...] = ot.T.astype(BF16)


def _flash(q, segs, lam_p, subg, score_bound, lam_init):
    b, lq, d = q.shape
    vd = d // N_HEADS
    tq = min(FLASH_TQ, lq)
    ins = [score_bound.reshape(1), lam_p, subg, q]
    specs = [pl.BlockSpec(memory_space=pltpu.SMEM),
             pl.BlockSpec(lam_p.shape, lambda bb, h, i: (0, 0)),
             pl.BlockSpec(subg.shape, lambda bb, h, i: (0, 0)),
             pl.BlockSpec((None, tq, vd), lambda bb, h, i: (bb, i, h))]
    for k, vt in segs:
        nk = k.shape[1]
        ins += [k, vt]
        specs += [pl.BlockSpec((None, nk, vd), lambda bb, h, i: (bb, 0, h)),
                  pl.BlockSpec((None, vd, nk), lambda bb, h, i: (bb, h, 0))]

    def call(online):
        kern = functools.partial(_flash_kernel, n_seg=len(segs), tk=FLASH_TK, lam_init=lam_init, online=online)
        return pl.pallas_call(
            kern,
            grid=(b, N_HEADS, lq // tq),
            in_specs=specs,
            out_specs=pl.BlockSpec((None, tq, vd), lambda bb, h, i: (bb, i, h)),
            out_shape=jax.ShapeDtypeStruct((b, lq, d), BF16),
            compiler_params=_params(3),
            name="diff_flash_online" if online else "diff_flash",
        )

    return lax.cond(score_bound <= MAX_FIXED_SHIFT, lambda *a: call(False)(*a), lambda *a: call(True)(*a), *ins)


def _conf_in_kernel(x_ref, mod_ref, g_ref, w_ref, b_ref, u_ref, *, mod_base):
    x = x_ref[...]
    d = x.shape[1]
    shift, scale, _ = _mod_rows(mod_ref, mod_base)
    h = _ada_in(x, g_ref[...], shift, scale).astype(BF16)
    for c in range(0, d, PROJ_CHUNK):
        a = _dot(h, w_ref[:, c:c + PROJ_CHUNK]) + b_ref[:, c:c + PROJ_CHUNK]
        gt = _dot(h, w_ref[:, d + c:d + c + PROJ_CHUNK]) + b_ref[:, d + c:d + c + PROJ_CHUNK]
        u_ref[:, c:c + PROJ_CHUNK] = a * _sigmoid(gt)


def _conf_in(x, mods, g, w_in, b_in, mod_base):
    b, n, d = x.shape
    tm = _ffn_tile(n)
    kern = functools.partial(_conf_in_kernel, mod_base=mod_base)
    return pl.pallas_call(
        kern,
        grid=(b, n // tm),
        in_specs=[_tok_spec(tm, d), _mod_spec(mods), _const_spec(g), _const_spec(w_in), _const_spec(b_in)],
        out_specs=_tok_spec(tm, d),
        out_shape=jax.ShapeDtypeStruct(x.shape, F32),
        compiler_params=_params(2),
        name="conf_in",
    )(x, mods, g, w_in, b_in)


def _sc_in_kernel(x_ref, mod_ref, g_ref, w_ref, b_out_ref, m_out_ref, *, mod_base):
    x = x_ref[...]
    d = x.shape[1]
    shift, scale, _ = _mod_rows(mod_ref, mod_base)
    h = _ada_in(x, g_ref[...], shift, scale).astype(BF16)
    for c in range(0, d, PROJ_CHUNK):
        b_out_ref[:, c:c + PROJ_CHUNK] = _dot(h, w_ref[:, c:c + PROJ_CHUNK])
        cg = _dot(h, w_ref[:, d + c:d + c + PROJ_CHUNK])
        xh = _dot(h, w_ref[:, 2 * d + c:2 * d + c + PROJ_CHUNK])
        m_out_ref[:, c:c + PROJ_CHUNK] = cg * xh


def _sc_in(x, mods, g, w_in, mod_base):
    b, n, d = x.shape
    tm = _ffn_tile(n)
    kern = functools.partial(_sc_in_kernel, mod_base=mod_base)
    return pl.pallas_call(
        kern,
        grid=(b, n // tm),
        in_specs=[_tok_spec(tm, d), _mod_spec(mods), _const_spec(g), _const_spec(w_in)],
        out_specs=[_tok_spec(tm, d), _tok_spec(tm, d)],
        out_shape=[jax.ShapeDtypeStruct(x.shape, F32)] * 2,
        compiler_params=_params(2),
        name="sconv_in",
    )(x, mods, g, w_in)


def _fill_conv_window(cwin_ref, cur_ref, prev_ref, next_ref, halo):
    tm = cur_ref.shape[0]
    i = pl.program_id(1)
    cwin_ref[0:halo] = jnp.where(i == 0, 0.0, prev_ref[...])
    cwin_ref[halo:halo + tm] = cur_ref[...]
    cwin_ref[halo + tm:] = jnp.where(i == pl.num_programs(1) - 1, 0.0, next_ref[...])


def _dwconv(cwin_ref, al_ref, cur_ref, prev_ref, next_ref, dw_ref, halo):
    tm = cur_ref.shape[0]
    taps = dw_ref.shape[0]
    _fill_conv_window(cwin_ref, cur_ref, prev_ref, next_ref, halo)
    off = halo - taps // 2
    acc = None
    for s in range(min(SUBLANES, taps)):
        ks = range(s, taps, SUBLANES)
        src, base = cwin_ref, off + s
        if len(ks) > 1 and base % SUBLANES:
            span = tm + SUBLANES * (len(ks) - 1)
            al_ref[0:span, :] = cwin_ref[base:base + span, :]
            src, base = al_ref, 0
        for jj, k in enumerate(ks):
            term = src[base + SUBLANES * jj:base + SUBLANES * jj + tm, :] * dw_ref[k:k + 1, :]
            acc = term if acc is None else acc + term
    return acc


def _halo_width(taps):
    return SUBLANES * (-(-(taps // 2) // SUBLANES))


def _halo_specs(tm, d, halo, n):
    r = tm // halo
    nb = n // halo
    prev = pl.BlockSpec((None, halo, d), lambda b, i: (b, jnp.maximum(i * r - 1, 0), 0))
    nxt = pl.BlockSpec((None, halo, d), lambda b, i: (b, jnp.minimum((i + 1) * r, nb - 1), 0))
    return prev, nxt


def _rope_tables(n_tokens, head_dim):
    rows = n_tokens // GRID_W
    row_ids = jnp.repeat(jnp.arange(rows, dtype=F32), GRID_W)
    col_ids = jnp.tile(jnp.arange(GRID_W, dtype=F32), rows)
    half = head_dim // 2
    inv_freq = ROPE_THETA ** (-jnp.arange(0, half, 2, dtype=F32) / half)
    ang_r = row_ids[:, None] * inv_freq
    ang_c = col_ids[:, None] * inv_freq
    ang = jnp.concatenate([ang_r, ang_r, ang_c, ang_c], axis=-1)
    quarter = head_dim // 4
    sign = jnp.where((jnp.arange(head_dim) % (2 * quarter)) < quarter, -1.0, 1.0).astype(F32)
    reps = LANES // head_dim
    return jnp.tile(jnp.cos(ang), (1, reps)), jnp.tile(jnp.sin(ang) * sign, (1, reps))


def _group_mean_matrix(head_dim):
    idx = jnp.arange(MXU_DIM) // head_dim
    return jnp.where(idx[:, None] == idx[None, :], 1.0 / head_dim, 0.0).astype(BF16)


def kernel(x, c, ctx, c_ctx, ada_w, ada_b, norm_g, ffn_w_in, ffn_w_out, attn_w_qkv, attn_w_o, attn_q_g,
           attn_k_g, attn_lambda, attn_subln_g, conv_w_in, conv_b_in, conv_dw_w, conv_dw_b, conv_ln_g,
           conv_ln_b, conv_w_out, conv_b_out, sc_w_in, sc_dw_w, sc_w_out):
    bsz, seq, d = x.shape
    depth = ada_w.shape[0]
    head_dim = d // (2 * N_HEADS)
    row = lambda v: v.reshape(1, -1)

    n_rows = ((bsz + 1 + 7) // 8) * 8
    cvec = jnp.zeros((n_rows, d), F32).at[:bsz].set(c).at[bsz].set(c_ctx)
    mods = _modulations(cvec, ada_w, ada_b).reshape(depth, n_rows, N_MOD, d)

    rope = _rope_tables(seq, head_dim)
    gmat = _group_mean_matrix(head_dim)
    ffn_w_in_bf = ffn_w_in.astype(BF16)
    ffn_w_out_bf = ffn_w_out.astype(BF16)

    for i in range(depth):
        kind = i % N_MIXERS
        j = i // N_MIXERS
        last = i == depth - 1
        ctx_in = (not last) or kind == 0
        ctx_out = not last
        ml = mods[i, :bsz]
        mc = mods[i, bsz:bsz + 1]
        g = [row(norm_g[i, s]) for s in range(3)]
        w_in = [(ffn_w_in_bf, (i, s)) for s in range(2)]
        w_out = [(ffn_w_out_bf, (i, s)) for s in range(2)]

        x = _ffn(x, ml, g[0], w_in[0], w_out[0], 0)
        if ctx_in:
            ctx = _ffn(ctx, mc, g[0], w_in[0], w_out[0], 0)

        if kind == 0:
            lam_init = 0.8 - 0.6 * math.exp(-0.3 * i)
            w_qkv = attn_w_qkv[j].astype(BF16)
            w_o = attn_w_o[j].astype(BF16)
            qg = jnp.tile(attn_q_g[j], MXU_DIM // head_dim).reshape(1, MXU_DIM)
            kg = jnp.tile(attn_k_g[j], MXU_DIM // head_dim).reshape(1, MXU_DIM)
            subg = attn_subln_g[j].reshape(-1, 1)
            bound = (1.02 * LOG2E * head_dim ** 0.5) * jnp.max(jnp.abs(attn_q_g[j])) * jnp.max(jnp.abs(attn_k_g[j]))
            q_l, k_l, vt_l = _qkv_proj(x, ml, g[1], w_qkv, gmat, qg, kg, rope, 3)
            q_c, k_c, vt_c = _qkv_proj(ctx, mc, g[1], w_qkv, gmat, qg, kg, None, 3)
            o_l = _flash(q_l, [(k_l, vt_l), (k_c, vt_c)], attn_lambda[j], subg, bound, lam_init)
            tail = lambda s, o, md: _tail_ffn("attn", s, [o], None, md, (w_o,), g[2], w_in[1], w_out[1], 3)
            x = tail(x, o_l, ml)
            if ctx_out:
                o_c = _flash(q_c, [(k_c, vt_c)], attn_lambda[j], subg, bound, lam_init)
                ctx = tail(ctx, o_c, mc)
        elif kind == 1:
            cw_in = conv_w_in[j].astype(BF16)
            tconsts = (conv_dw_w[j], row(conv_dw_b[j]), row(conv_ln_g[j]), row(conv_ln_b[j]),
                       conv_w_out[j].astype(BF16), row(conv_b_out[j]))

            def mixer(s, md):
                u = _conf_in(s, md, g[1], cw_in, row(conv_b_in[j]), 3)
                return _tail_ffn("conformer", s, [], u, md, tconsts, g[2], w_in[1], w_out[1], 3)

            x = mixer(x, ml)
            if ctx_out:
                ctx = mixer(ctx, mc)
        else:
            sw_in = sc_w_in[j].astype(BF16)
            tconsts = (sc_dw_w[j], sc_w_out[j].astype(BF16))

            def mixer(s, md):
                bg, m = _sc_in(s, md, g[1], sw_in, 3)
                return _tail_ffn("sconv", s, [bg], m, md, tconsts, g[2], w_in[1], w_out[1], 3)

            x = mixer(x, ml)
            if ctx_out:
                ctx = mixer(ctx, mc)

    return x
```
